```python
import jax, jax.numpy as jnp
from jax import lax
import numpy as np


D_MODEL = 4096
BATCH = 2
SEQ = 8192
DEPTH = 1

GRID_W = 64
ATTN_HEADS = 16
HEAD_DIM = 128
ATTN_WIDTH = ATTN_HEADS * HEAD_DIM
WIN_ROWS = 8
WIN_COLS = 16
CONV_WIDTH = 2048
CONV_GROUPS = 16
CONV_K = 3
N_BRANCHES = 2
IN_COLS = 3 * ATTN_WIDTH + 3 * CONV_WIDTH + N_BRANCHES * D_MODEL
PEER_HEADS = 8
PEER_KEY_DIM = 256
N_KEYS = 128
N_EXPERTS = N_KEYS * N_KEYS
PEER_TOPK = 16
PEER_CHUNK = 128
RMS_EPS = 1e-6
N_MOD = 6

kernel_name = 'hybrid_na2d_shortconv_peer_block'


def rmsnorm(x, g):
    xf = x.astype(jnp.float32)
    y = xf * lax.rsqrt(jnp.mean(xf * xf, axis=-1, keepdims=True) + RMS_EPS)
    return (y * g.astype(jnp.float32)).astype(x.dtype)


def modulate(h, shift, scale):
    return h * (1 + scale[:, None, :]) + shift[:, None, :]


def neighbourhood_attention(q, k, v, rpb):
    b, s, h, d = q.shape
    rows = s // GRID_W
    kh = min(WIN_ROWS, rows)
    kg = k.reshape(b, rows, GRID_W, h, d).transpose(0, 3, 1, 2, 4)
    vg = v.reshape(b, rows, GRID_W, h, d).transpose(0, 3, 1, 2, 4)
    qg = q.reshape(b, rows, GRID_W, h, d).transpose(1, 0, 3, 2, 4)
    cols = jnp.arange(GRID_W)
    col_start = jnp.clip(cols - WIN_COLS // 2, 0, GRID_W - WIN_COLS)
    col_idx = col_start[:, None] + jnp.arange(WIN_COLS)[None, :]
    dc = col_idx - cols[:, None] + (WIN_COLS - 1)
    scale = HEAD_DIM ** -0.5

    def row_step(args):
        r, q_row = args
        rs = jnp.clip(r - kh // 2, 0, rows - kh)
        k_rows = lax.dynamic_slice_in_dim(kg, rs, kh, axis=2)
        v_rows = lax.dynamic_slice_in_dim(vg, rs, kh, axis=2)
        k_win = k_rows[:, :, :, col_idx, :]
        v_win = v_rows[:, :, :, col_idx, :]
        dr = rs + jnp.arange(kh) - r + (WIN_ROWS - 1)
        bias = rpb[:, dr[None, :, None], dc[:, None, :]]
        sc = jnp.einsum('bhqd,bhiqjd->bhqij', q_row, k_win).astype(jnp.float32) * scale
        sc = sc + bias.astype(jnp.float32)[None]
        p = jax.nn.softmax(sc.reshape(b, h, GRID_W, kh * WIN_COLS), axis=-1)
        p = p.reshape(b, h, GRID_W, kh, WIN_COLS).astype(v.dtype)
        return jnp.einsum('bhqij,bhiqjd->bhqd', p, v_win)

    out = lax.map(row_step, (jnp.arange(rows), qg))
    return out.transpose(1, 0, 3, 2, 4).reshape(b, s, h * d)


def short_conv_mixer(hin, gate_b, gate_c, conv_w):
    u = gate_c * hin
    y = lax.conv_general_dilated(
        u, conv_w[:, None, :], window_strides=(1,),
        padding=[(CONV_K // 2, CONV_K // 2)],
        dimension_numbers=('NWC', 'WIO', 'NWC'),
        feature_group_count=u.shape[-1])
    return gate_b * y


def peer(h, w_q, sub_keys_1, sub_keys_2, expert_u, expert_v):
    b, s, dm = h.shape
    n_tok = b * s
    t = h.reshape(n_tok, dm)
    q = (t @ w_q).reshape(n_tok, PEER_HEADS, 2, PEER_KEY_DIM // 2)
    s1 = jnp.einsum('thd,hkd->thk', q[:, :, 0], sub_keys_1)
    s2 = jnp.einsum('thd,hkd->thk', q[:, :, 1], sub_keys_2)
    v1, i1 = lax.top_k(s1, PEER_TOPK)
    v2, i2 = lax.top_k(s2, PEER_TOPK)
    cand = (v1[..., :, None] + v2[..., None, :]).reshape(n_tok, PEER_HEADS, PEER_TOPK * PEER_TOPK)
    cand_idx = (i1[..., :, None] * N_KEYS + i2[..., None, :]).reshape(n_tok, PEER_HEADS, PEER_TOPK * PEER_TOPK)
    top_s, pos = lax.top_k(cand, PEER_TOPK)
    idx = jnp.take_along_axis(cand_idx, pos, axis=-1)
    g = jax.nn.softmax(top_s.astype(jnp.float32), axis=-1).astype(h.dtype)
    n_chunks = n_tok // PEER_CHUNK

    def chunk_step(args):
        tc, ic, gc = args
        u = jnp.take(expert_u, ic, axis=0)
        vv = jnp.take(expert_v, ic, axis=0)
        a = jax.nn.gelu(jnp.einsum('cd,ced->ce', tc, u))
        return jnp.einsum('ce,ced->cd', gc * a, vv)

    out = lax.map(chunk_step, (t.reshape(n_chunks, PEER_CHUNK, dm),
                               idx.reshape(n_chunks, PEER_CHUNK, PEER_HEADS * PEER_TOPK),
                               g.reshape(n_chunks, PEER_CHUNK, PEER_HEADS * PEER_TOPK)))
    return out.reshape(b, s, dm)


def setup_inputs(seed: int = 0) -> dict:
    key = jax.random.key(seed)
    ks = jax.random.split(key, 18)
    f32 = jnp.float32

    def nrm(k, shape, s):
        return jax.random.normal(k, shape, f32) * s

    return {
        'x': nrm(ks[0], (BATCH, SEQ, D_MODEL), 1.0),
        'c': nrm(ks[1], (BATCH, D_MODEL), 1.0),
        'w_ada': nrm(ks[2], (DEPTH, D_MODEL, N_MOD * D_MODEL), 0.5 * D_MODEL ** -0.5),
        'b_ada': nrm(ks[3], (DEPTH, N_MOD * D_MODEL), 0.01),
        'norm1_g': 1.0 + nrm(ks[4], (DEPTH, D_MODEL), 0.01),
        'w_in': nrm(ks[5], (DEPTH, D_MODEL, IN_COLS), D_MODEL ** -0.5),
        'rpb': nrm(ks[6], (DEPTH, ATTN_HEADS, 2 * WIN_ROWS - 1, 2 * WIN_COLS - 1), 0.5),
        'conv_w': nrm(ks[7], (DEPTH, CONV_K, CONV_WIDTH), CONV_K ** -0.5),
        'w_attn_out': nrm(ks[8], (DEPTH, ATTN_WIDTH, D_MODEL), ATTN_WIDTH ** -0.5),
        'w_conv_out': nrm(ks[9], (DEPTH, CONV_WIDTH, D_MODEL), CONV_WIDTH ** -0.5),
        'w_o': nrm(ks[10], (DEPTH, D_MODEL, D_MODEL), D_MODEL ** -0.5),
        'norm2_g': 1.0 + nrm(ks[11], (DEPTH, D_MODEL), 0.01),
        'w_q_peer': nrm(ks[12], (DEPTH, D_MODEL, PEER_HEADS * PEER_KEY_DIM), D_MODEL ** -0.5),
        'sub_keys_1': nrm(ks[13], (DEPTH, PEER_HEADS, N_KEYS, PEER_KEY_DIM // 2), (PEER_KEY_DIM // 2) ** -0.5),
        'sub_keys_2': nrm(ks[14], (DEPTH, PEER_HEADS, N_KEYS, PEER_KEY_DIM // 2), (PEER_KEY_DIM // 2) ** -0.5),
        'expert_u': nrm(ks[15], (DEPTH, N_EXPERTS, D_MODEL), D_MODEL ** -0.5),
        'expert_v': nrm(ks[16], (DEPTH, N_EXPERTS, D_MODEL), PEER_HEADS ** -0.5),
        'norm_f_g': 1.0 + nrm(ks[17], (D_MODEL,), 0.01),
    }


def reference(x, c, w_ada, b_ada, norm1_g, w_in, rpb, conv_w, w_attn_out, w_conv_out,
              w_o, norm2_g, w_q_peer, sub_keys_1, sub_keys_2, expert_u, expert_v, norm_f_g):
    b, s, _ = x.shape
    splits = np.cumsum([ATTN_WIDTH, ATTN_WIDTH, ATTN_WIDTH,
                        CONV_WIDTH, CONV_WIDTH, CONV_WIDTH, D_MODEL]).tolist()
    for l in range(DEPTH):
        mod = c @ w_ada[l] + b_ada[l]
        shift1, scale1, gate1, shift2, scale2, gate2 = jnp.split(mod, N_MOD, axis=-1)

        h1 = modulate(rmsnorm(x, norm1_g[l]), shift1, scale1)
        proj = h1 @ w_in[l]
        q, k, v, cb, cc, ch, ga, gb = jnp.split(proj, splits, axis=-1)
        hs = (b, s, ATTN_HEADS, HEAD_DIM)
        attn = neighbourhood_attention(q.reshape(hs), k.reshape(hs), v.reshape(hs), rpb[l])
        y_a = attn @ w_attn_out[l]
        y_b = short_conv_mixer(ch, cb, cc, conv_w[l]) @ w_conv_out[l]
        merged = jax.nn.sigmoid(ga) * y_a + jax.nn.sigmoid(gb) * y_b
        x = x + gate1[:, None, :] * (merged @ w_o[l])

        h2 = modulate(rmsnorm(x, norm2_g[l]), shift2, scale2)
        x = x + gate2[:, None, :] * peer(h2, w_q_peer[l], sub_keys_1[l], sub_keys_2[l],
                                         expert_u[l], expert_v[l])
    return rmsnorm(x, norm_f_g)
```

```python
import functools

import numpy as np
import jax
import jax.numpy as jnp
from jax import lax
from jax.experimental import pallas as pl
from jax.experimental.pallas import tpu as pltpu

GRID_W = 64
RMS_EPS = 1e-6
PEER_TOPK = 16
LANES = 128
BF16_ROWS = 16
V7X_VMEM_BYTES = 64 * 1024 * 1024
VMEM_LIMIT = V7X_VMEM_BYTES - 8 * 1024 * 1024
ATTN_ROW_BLOCK = 8
MASKED = -1e30

F32 = jnp.float32
BF16 = jnp.bfloat16


def _cparams(*sem):
    return pltpu.CompilerParams(dimension_semantics=sem, vmem_limit_bytes=VMEM_LIMIT)


def _tile(n, target, align=LANES):
    best = None
    for cand in range(align, min(n, target) + 1, align):
        if n % cand == 0:
            best = cand
    assert best is not None, (n, target, align)
    return best


def _mod_kernel(cb_ref, w_ref, b_ref, o_ref):
    nb = cb_ref.shape[0]
    tn = w_ref.shape[1]
    for j in range(tn // LANES):
        cols = slice(j * LANES, (j + 1) * LANES)
        w = w_ref[:, cols]
        for b in range(nb):
            o_ref[b:b + 1, cols] = jnp.sum(w * cb_ref[b], axis=0, keepdims=True) + b_ref[:, cols]


def _mod(c, w_ada, b_ada, tn=512):
    nb, d = c.shape
    n = w_ada.shape[1]
    cb = jnp.broadcast_to(c[:, :, None], (nb, d, LANES))
    return pl.pallas_call(
        _mod_kernel,
        grid=(n // tn,),
        in_specs=[pl.BlockSpec((nb, d, LANES), lambda j: (0, 0, 0)),
                  pl.BlockSpec((d, tn), lambda j: (0, j)),
                  pl.BlockSpec((1, tn), lambda j: (0, j))],
        out_specs=pl.BlockSpec((nb, tn), lambda j: (0, j)),
        out_shape=jax.ShapeDtypeStruct((nb, n), F32),
        compiler_params=_cparams("arbitrary"),
        name="ada_mod",
    )(cb, w_ada, b_ada.reshape(1, n))


def _norm_mod_kernel(x_ref, g_ref, mod_ref, o_ref, *, shift_row):
    x = x_ref[...]
    y = x * lax.rsqrt(jnp.mean(x * x, axis=-1, keepdims=True) + RMS_EPS) * g_ref[...]
    shift = mod_ref[shift_row:shift_row + 1, :]
    scale = mod_ref[shift_row + 1:shift_row + 2, :]
    o_ref[...] = (y * (1.0 + scale) + shift).astype(o_ref.dtype)


def _norm_mod(x2, g, mod3, shift_row, seq, tm=256):
    t, d = x2.shape
    per_seq = seq // tm
    return pl.pallas_call(
        functools.partial(_norm_mod_kernel, shift_row=shift_row),
        grid=(t // tm,),
        in_specs=[pl.BlockSpec((tm, d), lambda i: (i, 0)),
                  pl.BlockSpec((1, d), lambda i: (0, 0)),
                  pl.BlockSpec((None, mod3.shape[1], d), lambda i: (i // per_seq, 0, 0))],
        out_specs=pl.BlockSpec((tm, d), lambda i: (i, 0)),
        out_shape=jax.ShapeDtypeStruct((t, d), BF16),
        compiler_params=_cparams("parallel"),
        name="norm_mod",
    )(x2, g.reshape(1, d), mod3)


def _mm_kernel(a_ref, w_ref, o_ref):
    o_ref[...] = jnp.dot(a_ref[...], w_ref[...], preferred_element_type=F32).astype(o_ref.dtype)


def _matmul(a, w, tm, tn, name):
    m, k = a.shape
    n = w.shape[1]
    tm, tn = _tile(m, tm), _tile(n, tn)
    return pl.pallas_call(
        _mm_kernel,
        grid=(m // tm, n // tn),
        in_specs=[pl.BlockSpec((tm, k), lambda i, j: (i, 0)),
                  pl.BlockSpec((k, tn), lambda i, j: (0, j))],
        out_specs=pl.BlockSpec((tm, tn), lambda i, j: (i, j)),
        out_shape=jax.ShapeDtypeStruct((m, n), BF16),
        compiler_params=_cparams("parallel", "arbitrary"),
        name=name,
    )(a, w)


def _attn_geometry(rows, win_rows):
    rb = ATTN_ROW_BLOCK
    kh = min(win_rows, rows)
    kr = min(rb + kh, rows)
    nrb = rows // rb
    assert rows % rb == 0 and rb >= kh // 2, (rows, rb, kh)
    return rb, kh, kr, nrb


def _window_start(r0, kh, kr, rows):
    return jnp.clip(r0 - kh // 2, 0, rows - kr)


def _attn_bias_table(rpb, rows):
    nh, nr, nc = rpb.shape
    win_rows, win_cols = (nr + 1) // 2, (nc + 1) // 2
    rb, kh, kr, nrb = _attn_geometry(rows, win_rows)
    reps = [0, min(1, nrb - 1), nrb - 1]
    a = np.arange(rb)
    ip = np.arange(kr)
    dr = np.zeros((3, rb, kr), np.int32)
    rvalid = np.zeros((3, rb, kr), bool)
    for kind, rbi in enumerate(reps):
        r0 = rbi * rb
        ws = int(np.clip(r0 - kh // 2, 0, rows - kr))
        r = r0 + a[:, None]
        krow = ws + ip[None, :]
        rs = np.clip(r - kh // 2, 0, rows - kh)
        rvalid[kind] = (krow >= rs) & (krow < rs + kh)
        dr[kind] = np.clip(krow - r + (win_rows - 1), 0, nr - 1)
    cols = np.arange(GRID_W)
    cs = np.clip(cols - win_cols // 2, 0, GRID_W - win_cols)
    kc = cols[None, :]
    cvalid = (kc >= cs[:, None]) & (kc < cs[:, None] + win_cols)
    dc = np.clip(kc - cols[:, None] + (win_cols - 1), 0, nc - 1)
    onehot = (dc[None, :, :] == np.arange(nc)[:, None, None]).astype(np.float32)
    r1 = rpb[:, dr, :]
    bias = jnp.einsum('hkaim,mqc->hkaqic', r1, jnp.asarray(onehot), precision=lax.Precision.HIGHEST)
    valid = rvalid[:, :, None, :, None] & cvalid[None, None, :, None, :]
    bias = jnp.where(jnp.asarray(valid)[None], bias, MASKED)
    return bias.reshape(nh, 3, rb * GRID_W, kr * GRID_W)


def _attn_kernel(q_ref, k_ref, v_ref, bias_ref, o_ref, *, rows, kh, kr, rb, scale):
    r0 = pl.program_id(2) * rb
    start = pl.multiple_of(_window_start(r0, kh, kr, rows) * GRID_W, GRID_W)
    nk = kr * GRID_W
    k = k_ref[pl.ds(start, nk), :]
    v = v_ref[pl.ds(start, nk), :]
    s = lax.dot_general(q_ref[...], k, (((1,), (1,)), ((), ())), preferred_element_type=F32)
    s = s * scale + bias_ref[...]
    p = jnp.exp(s - jnp.max(s, axis=-1, keepdims=True))
    denom = jnp.sum(p, axis=-1, keepdims=True)
    o = jnp.dot(p.astype(v.dtype), v, preferred_element_type=F32)
    o_ref[...] = (o / denom).astype(o_ref.dtype)


def _attention(proj, bias, nb, seq, n_heads, head_dim, aw, win_rows):
    rows = seq // GRID_W
    rb, kh, kr, nrb = _attn_geometry(rows, win_rows)
    tq = rb * GRID_W
    hoff = aw // head_dim

    def kind(r):
        return jnp.where(r == 0, 0, jnp.where(r == nrb - 1, 2, 1))

    return pl.pallas_call(
        functools.partial(_attn_kernel, rows=rows, kh=kh, kr=kr, rb=rb, scale=head_dim ** -0.5),
        grid=(n_heads, nb, nrb),
        in_specs=[pl.BlockSpec((tq, head_dim), lambda h, b, r: (b * nrb + r, h)),
                  pl.BlockSpec((seq, head_dim), lambda h, b, r: (b, hoff + h)),
                  pl.BlockSpec((seq, head_dim), lambda h, b, r: (b, 2 * hoff + h)),
                  pl.BlockSpec((None, None, tq, kr * GRID_W), lambda h, b, r: (h, kind(r), 0, 0))],
        out_specs=pl.BlockSpec((tq, head_dim), lambda h, b, r: (b * nrb + r, h)),
        out_shape=jax.ShapeDtypeStruct((nb * seq, aw), BF16),
        compiler_params=_cparams("parallel", "parallel", "arbitrary"),
        name="na2d_attention",
    )(proj, proj, proj, bias)


def _conv_kernel(cb_ref, cc_ref, ch_ref, ccp_ref, chp_ref, ccn_ref, chn_ref, w_ref, o_ref, *, per_seq):
    i = pl.program_id(0)
    ts = cc_ref.shape[0]
    u = cc_ref[...].astype(F32) * ch_ref[...].astype(F32)
    last_halo = BF16_ROWS - 1
    u_before = ccp_ref[last_halo:, :].astype(F32) * chp_ref[last_halo:, :].astype(F32)
    u_after = ccn_ref[0:1, :].astype(F32) * chn_ref[0:1, :].astype(F32)
    u_before = jnp.where(i % per_seq == 0, 0.0, u_before)
    u_after = jnp.where(i % per_seq == per_seq - 1, 0.0, u_after)
    row = lax.broadcasted_iota(jnp.int32, u.shape, 0)
    u_prev = jnp.where(row == 0, u_before, pltpu.roll(u, 1, axis=0))
    u_next = jnp.where(row == ts - 1, u_after, pltpu.roll(u, ts - 1, axis=0))
    y = w_ref[0:1, :] * u_prev + w_ref[1:2, :] * u + w_ref[2:3, :] * u_next
    o_ref[...] = (cb_ref[...].astype(F32) * y).astype(o_ref.dtype)


def _short_conv(proj, conv_w, seq, col0, cw, ts=512, tc=512):
    t = proj.shape[0]
    ts, tc = _tile(seq, ts), _tile(np.gcd(cw, col0), tc)
    per_seq = seq // ts
    c0 = col0 // tc
    nct = cw // tc
    hb = ts // BF16_ROWS
    last_hb = t // BF16_ROWS - 1

    def main(k):
        return pl.BlockSpec((ts, tc), lambda i, j: (i, c0 + k * nct + j))

    def before(k):
        return pl.BlockSpec((BF16_ROWS, tc), lambda i, j: (jnp.maximum(i * hb - 1, 0), c0 + k * nct + j))

    def after(k):
        return pl.BlockSpec((BF16_ROWS, tc), lambda i, j: (jnp.minimum((i + 1) * hb, last_hb), c0 + k * nct + j))

    return pl.pallas_call(
        functools.partial(_conv_kernel, per_seq=per_seq),
        grid=(t // ts, nct),
        in_specs=[main(0), main(1), main(2), before(1), before(2), after(1), after(2),
                  pl.BlockSpec((conv_w.shape[0], tc), lambda i, j: (0, j))],
        out_specs=pl.BlockSpec((ts, tc), lambda i, j: (i, j)),
        out_shape=jax.ShapeDtypeStruct((t, cw), BF16),
        compiler_params=_cparams("parallel", "parallel"),
        name="short_conv",
    )(proj, proj, proj, proj, proj, proj, proj, conv_w)


def _merge_kernel(a_ref, c_ref, wa_ref, wb_ref, ga_ref, gb_ref, o_ref):
    ya = jnp.dot(a_ref[...], wa_ref[...], preferred_element_type=F32)
    yb = jnp.dot(c_ref[...], wb_ref[...], preferred_element_type=F32)
    ga = jax.nn.sigmoid(ga_ref[...].astype(F32))
    gb = jax.nn.sigmoid(gb_ref[...].astype(F32))
    o_ref[...] = (ga * ya + gb * yb).astype(o_ref.dtype)


def _merge(attn, convm, wa, wb, proj, gate_col0, tm=1024, tn=512):
    t, d = attn.shape[0], wa.shape[1]
    tm, tn = _tile(t, tm), _tile(np.gcd(d, gate_col0), tn)
    g0 = gate_col0 // tn
    nt = d // tn
    return pl.pallas_call(
        _merge_kernel,
        grid=(t // tm, nt),
        in_specs=[pl.BlockSpec((tm, attn.shape[1]), lambda i, j: (i, 0)),
                  pl.BlockSpec((tm, convm.shape[1]), lambda i, j: (i, 0)),
                  pl.BlockSpec((wa.shape[0], tn), lambda i, j: (0, j)),
                  pl.BlockSpec((wb.shape[0], tn), lambda i, j: (0, j)),
                  pl.BlockSpec((tm, tn), lambda i, j: (i, g0 + j)),
                  pl.BlockSpec((tm, tn), lambda i, j: (i, g0 + nt + j))],
        out_specs=pl.BlockSpec((tm, tn), lambda i, j: (i, j)),
        out_shape=jax.ShapeDtypeStruct((t, d), BF16),
        compiler_params=_cparams("parallel", "arbitrary"),
        name="merge_branches",
    )(attn, convm, wa, wb, proj, proj)


def _wo_kernel(m_ref, w_ref, x_ref, mod_ref, o_ref, *, gate_row):
    y = jnp.dot(m_ref[...], w_ref[...], preferred_element_type=F32)
    o_ref[...] = x_ref[...] + mod_ref[gate_row:gate_row + 1, :] * y


def _wo_residual(merged, wo, x2, mod3, gate_row, seq, tm=1024, tn=512):
    t, d = x2.shape
    tm, tn = _tile(seq, tm), _tile(d, tn)
    per_seq = seq // tm
    return pl.pallas_call(
        functools.partial(_wo_kernel, gate_row=gate_row),
        grid=(t // tm, d // tn),
        in_specs=[pl.BlockSpec((tm, merged.shape[1]), lambda i, j: (i, 0)),
                  pl.BlockSpec((wo.shape[0], tn), lambda i, j: (0, j)),
                  pl.BlockSpec((tm, tn), lambda i, j: (i, j)),
                  pl.BlockSpec((None, mod3.shape[1], tn), lambda i, j: (i // per_seq, 0, j))],
        out_specs=pl.BlockSpec((tm, tn), lambda i, j: (i, j)),
        out_shape=jax.ShapeDtypeStruct((t, d), F32),
        compiler_params=_cparams("parallel", "arbitrary"),
        name="wo_residual",
    )(merged, wo, x2, mod3)


def _candidate_pairs():
    k = PEER_TOPK
    return [(p, q) for p in range(k) for q in range(k) if (p + 1) * (q + 1) <= k]


def _top16(s, iota):
    n = s.shape[0]
    rank = jnp.full(s.shape, PEER_TOPK, jnp.int32)
    vals = []
    for it in range(PEER_TOPK):
        m = jnp.max(s, axis=0, keepdims=True)
        idx = jnp.min(jnp.where(s == m, iota, n), axis=0, keepdims=True)
        sel = iota == idx
        rank = jnp.where(sel, it, rank)
        s = jnp.where(sel, -jnp.inf, s)
        vals.append(m)
    return rank, vals


def _peer_select_kernel(q_ref, k1_ref, k2_ref, lim_ref, e1_ref, r2_ref, e2_ref):
    nh, n, dk = k1_ref.shape
    tm = q_ref.shape[0]
    iota = lax.broadcasted_iota(jnp.int32, (n, tm), 0)
    nt = (((1,), (1,)), ((), ()))
    a_rows, b_rows = [], []
    for h in range(nh):
        q1 = q_ref[:, (2 * h) * dk:(2 * h + 1) * dk]
        q2 = q_ref[:, (2 * h + 1) * dk:(2 * h + 2) * dk]
        s1 = lax.dot_general(k1_ref[h], q1, nt, preferred_element_type=F32)
        s2 = lax.dot_general(k2_ref[h], q2, nt, preferred_element_type=F32)
        rank1, a = _top16(s1, iota)
        rank2, b = _top16(s2, iota)
        e1_ref[h] = s1
        e2_ref[h] = s2
        lim_ref[h] = rank1.astype(F32)
        r2_ref[h] = rank2.astype(F32)
        a_rows.append(a)
        b_rows.append(b)
    big_a = [jnp.concatenate([a_rows[h][p] for h in range(nh)], axis=0) for p in range(PEER_TOPK)]
    big_b = [jnp.concatenate([b_rows[h][q] for h in range(nh)], axis=0) for q in range(PEER_TOPK)]
    pairs = _candidate_pairs()
    val = [big_a[p] + big_b[q] for p, q in pairs]
    beaten = []
    for c, (p, q) in enumerate(pairs):
        beaten.append(jnp.full((nh, tm), float((p + 1) * (q + 1) - 1), F32))
    for c1, (p1, q1) in enumerate(pairs):
        for c2 in range(c1 + 1, len(pairs)):
            p2, q2 = pairs[c2]
            if (p1 <= p2 and q1 <= q2) or (p2 <= p1 and q2 <= q1):
                continue
            first_wins = jnp.where(val[c1] >= val[c2], 1.0, 0.0)
            beaten[c2] = beaten[c2] + first_wins
            beaten[c1] = beaten[c1] + (1.0 - first_wins)
    count = [jnp.zeros((nh, tm), F32) for _ in range(PEER_TOPK)]
    z = jnp.zeros((nh, tm), F32)
    for c, (p, q) in enumerate(pairs):
        chosen = beaten[c] < float(PEER_TOPK)
        count[p] = count[p] + jnp.where(chosen, 1.0, 0.0)
        z = z + jnp.where(chosen, jnp.exp((big_a[p] - big_a[0]) + (big_b[q] - big_b[0])), 0.0)
    inv_z = 1.0 / z
    for h in range(nh):
        rank1 = lim_ref[h]
        lim = jnp.zeros((n, tm), F32)
        for p in range(PEER_TOPK):
            lim = jnp.where(rank1 == float(p), count[p][h:h + 1, :], lim)
        lim_ref[h] = lim
        e1_ref[h] = jnp.exp(e1_ref[h] - big_a[0][h:h + 1, :])
        e2_ref[h] = jnp.exp(e2_ref[h] - big_b[0][h:h + 1, :]) * inv_z[h:h + 1, :]


def _peer_select(qp, k1, k2, tm=256):
    t = qp.shape[0]
    nh, n, dk = k1.shape
    out = jax.ShapeDtypeStruct((nh, n, t), F32)
    ospec = pl.BlockSpec((nh, n, tm), lambda i: (0, 0, i))
    return pl.pallas_call(
        _peer_select_kernel,
        grid=(t // tm,),
        in_specs=[pl.BlockSpec((tm, qp.shape[1]), lambda i: (i, 0)),
                  pl.BlockSpec((nh, n, dk), lambda i: (0, 0, 0)),
                  pl.BlockSpec((nh, n, dk), lambda i: (0, 0, 0))],
        out_specs=[ospec, ospec, ospec, ospec],
        out_shape=[out, out, out, out],
        compiler_params=_cparams("parallel"),
        name="peer_select",
    )(qp, k1, k2)


def _peer_dense_kernel(h_ref, u_ref, vt_ref, lim_ref, e1_ref, r2_ref, e2_ref, o_ref):
    e = pl.program_id(1)
    te = u_ref.shape[0]
    tm = h_ref.shape[0]
    nh, n, _ = r2_ref.shape
    per_tile = te // n

    @pl.when(e == 0)
    def _():
        o_ref[...] = jnp.zeros_like(o_ref)

    act = jax.nn.gelu(lax.dot_general(u_ref[...], h_ref[...], (((1,), (1,)), ((), ())),
                                      preferred_element_type=F32))
    gates = []
    for c in range(per_tile):
        i = e * per_tile + c
        g = jnp.zeros((n, tm), F32)
        for h in range(nh):
            lim = lim_ref[h, pl.ds(i, 1), :]
            e1 = e1_ref[h, pl.ds(i, 1), :]
            g = g + jnp.where(r2_ref[h] < lim, e2_ref[h] * e1, 0.0)
        gates.append(g)
    ga = (jnp.concatenate(gates, axis=0) * act).astype(BF16)
    o_ref[...] += jnp.dot(vt_ref[...], ga, preferred_element_type=F32)


def _peer_dense(h2, u, vt, lim, e1, r2, e2, tm=256, te=512):
    t, d = h2.shape
    ne = u.shape[0]
    nh, n, _ = lim.shape
    sel = pl.BlockSpec((nh, n, tm), lambda i, e: (0, 0, i))
    return pl.pallas_call(
        _peer_dense_kernel,
        grid=(t // tm, ne // te),
        in_specs=[pl.BlockSpec((tm, d), lambda i, e: (i, 0)),
                  pl.BlockSpec((te, d), lambda i, e: (e, 0)),
                  pl.BlockSpec((d, te), lambda i, e: (0, e)),
                  sel, sel, sel, sel],
        out_specs=pl.BlockSpec((d, tm), lambda i, e: (0, i)),
        out_shape=jax.ShapeDtypeStruct((d, t), F32),
        compiler_params=_cparams("parallel", "arbitrary"),
        name="peer_dense",
    )(h2, u, vt, lim, e1, r2, e2)


def _peer_residual_kernel(x_ref, pt_ref, mod_ref, g_ref, o_ref, *, gate_row, final_norm):
    y = x_ref[...] + mod_ref[gate_row:gate_row + 1, :] * pt_ref[...].T
    if final_norm:
        y = y * lax.rsqrt(jnp.mean(y * y, axis=-1, keepdims=True) + RMS_EPS) * g_ref[...]
    o_ref[...] = y


def _peer_residual(x1, peer_t, mod3, gate_row, g, seq, final_norm, tm=256):
    t, d = x1.shape
    per_seq = seq // tm
    return pl.pallas_call(
        functools.partial(_peer_residual_kernel, gate_row=gate_row, final_norm=final_norm),
        grid=(t // tm,),
        in_specs=[pl.BlockSpec((tm, d), lambda i: (i, 0)),
                  pl.BlockSpec((d, tm), lambda i: (0, i)),
                  pl.BlockSpec((None, mod3.shape[1], d), lambda i: (i // per_seq, 0, 0)),
                  pl.BlockSpec((1, d), lambda i: (0, 0))],
        out_specs=pl.BlockSpec((tm, d), lambda i: (i, 0)),
        out_shape=jax.ShapeDtypeStruct((t, d), F32),
        compiler_params=_cparams("parallel"),
        name="peer_residual",
    )(x1, peer_t, mod3, g.reshape(1, d))


def kernel(x, c, w_ada, b_ada, norm1_g, w_in, rpb, conv_w, w_attn_out, w_conv_out, w_o, norm2_g,
           w_q_peer, sub_keys_1, sub_keys_2, expert_u, expert_v, norm_f_g):
    nb, seq, d = x.shape
    depth = w_ada.shape[0]
    n_mod = w_ada.shape[2] // d
    aw = w_attn_out.shape[1]
    cw = w_conv_out.shape[1]
    n_heads = rpb.shape[1]
    head_dim = aw // n_heads
    rows = seq // GRID_W
    x2 = x.reshape(nb * seq, d)
    for l in range(depth):
        mod3 = _mod(c, w_ada[l], b_ada[l]).reshape(nb, n_mod, d)

        h1 = _norm_mod(x2, norm1_g[l], mod3, 0, seq)
        proj = _matmul(h1, w_in[l].astype(BF16), 1024, 1024, "in_proj")
        bias = _attn_bias_table(rpb[l], rows)
        attn = _attention(proj, bias, nb, seq, n_heads, head_dim, aw, (rpb.shape[2] + 1) // 2)
        convm = _short_conv(proj, conv_w[l], seq, 3 * aw, cw)
        merged = _merge(attn, convm, w_attn_out[l].astype(BF16), w_conv_out[l].astype(BF16),
                        proj, 3 * aw + 3 * cw)
        x2 = _wo_residual(merged, w_o[l].astype(BF16), x2, mod3, 2, seq)

        h2 = _norm_mod(x2, norm2_g[l], mod3, 3, seq)
        qp = _matmul(h2, w_q_peer[l].astype(BF16), 1024, 1024, "peer_query")
        lim, e1, r2, e2 = _peer_select(qp, sub_keys_1[l].astype(BF16), sub_keys_2[l].astype(BF16))
        peer_t = _peer_dense(h2, expert_u[l].astype(BF16), expert_v[l].T.astype(BF16), lim, e1, r2, e2)
        x2 = _peer_residual(x2, peer_t, mod3, 5, norm_f_g, seq, final_norm=(l == depth - 1))
    return x2.reshape(nb, seq, d)
```

```python
import functools

import numpy as np
import jax
import jax.numpy as jnp
from jax import lax
from jax.experimental import pallas as pl
from jax.experimental.pallas import tpu as pltpu

GRID_W = 64
RMS_EPS = 1e-6
PEER_TOPK = 16
LANES = 128
BF16_ROWS = 16
V7X_VMEM_BYTES = 64 * 1024 * 1024
VMEM_LIMIT = V7X_VMEM_BYTES - 8 * 1024 * 1024
ATTN_ROW_BLOCK = 8
MASKED = -1e30

F32 = jnp.float32
BF16 = jnp.bfloat16


def _cparams(*sem):
    return pltpu.CompilerParams(dimension_semantics=sem, vmem_limit_bytes=VMEM_LIMIT)


def _tile(n, target, align=LANES):
    best = None
    for cand in range(align, min(n, target) + 1, align):
        if n % cand == 0:
            best = cand
    assert best is not None, (n, target, align)
    return best


def _mod_kernel(cb_ref, w_ref, b_ref, o_ref):
    nb = cb_ref.shape[0]
    tn = w_ref.shape[1]
    for j in range(tn // LANES):
        cols = slice(j * LANES, (j + 1) * LANES)
        w = w_ref[:, cols]
        for b in range(nb):
            o_ref[b:b + 1, cols] = jnp.sum(w * cb_ref[b], axis=0, keepdims=True) + b_ref[:, cols]


def _mod(c, w_ada, b_ada, tn=512):
    nb, d = c.shape
    n = w_ada.shape[1]
    cb = jnp.broadcast_to(c[:, :, None], (nb, d, LANES))
    return pl.pallas_call(
        _mod_kernel,
        grid=(n // tn,),
        in_specs=[pl.BlockSpec((nb, d, LANES), lambda j: (0, 0, 0)),
                  pl.BlockSpec((d, tn), lambda j: (0, j)),
                  pl.BlockSpec((1, tn), lambda j: (0, j))],
        out_specs=pl.BlockSpec((nb, tn), lambda j: (0, j)),
        out_shape=jax.ShapeDtypeStruct((nb, n), F32),
        compiler_params=_cparams("arbitrary"),
        name="ada_mod",
    )(cb, w_ada, b_ada.reshape(1, n))


def _norm_mod_kernel(x_ref, g_ref, mod_ref, o_ref, *ot_ref, shift_row):
    x = x_ref[...]
    y = x * lax.rsqrt(jnp.mean(x * x, axis=-1, keepdims=True) + RMS_EPS) * g_ref[...]
    shift = mod_ref[shift_row:shift_row + 1, :]
    scale = mod_ref[shift_row + 1:shift_row + 2, :]
    h = y * (1.0 + scale) + shift
    o_ref[...] = h.astype(o_ref.dtype)
    if ot_ref:
        ot_ref[0][...] = h.T.astype(o_ref.dtype)


def _norm_mod(x2, g, mod3, shift_row, seq, with_transposed=False, tm=256):
    t, d = x2.shape
    per_seq = seq // tm
    out_specs = [pl.BlockSpec((tm, d), lambda i: (i, 0))]
    out_shape = [jax.ShapeDtypeStruct((t, d), BF16)]
    if with_transposed:
        out_specs.append(pl.BlockSpec((d, tm), lambda i: (0, i)))
        out_shape.append(jax.ShapeDtypeStruct((d, t), BF16))
    out = pl.pallas_call(
        functools.partial(_norm_mod_kernel, shift_row=shift_row),
        grid=(t // tm,),
        in_specs=[pl.BlockSpec((tm, d), lambda i: (i, 0)),
                  pl.BlockSpec((1, d), lambda i: (0, 0)),
                  pl.BlockSpec((None, mod3.shape[1], d), lambda i: (i // per_seq, 0, 0))],
        out_specs=out_specs,
        out_shape=out_shape,
        compiler_params=_cparams("parallel"),
        name="norm_mod",
    )(x2, g.reshape(1, d), mod3)
    return out if with_transposed else out[0]


def _mm_kernel(a_ref, w_ref, o_ref):
    o_ref[...] = jnp.dot(a_ref[...], w_ref[...], preferred_element_type=F32).astype(o_ref.dtype)


def _matmul(a, w, tm, tn, name):
    m, k = a.shape
    n = w.shape[1]
    tm, tn = _tile(m, tm), _tile(n, tn)
    return pl.pallas_call(
        _mm_kernel,
        grid=(m // tm, n // tn),
        in_specs=[pl.BlockSpec((tm, k), lambda i, j: (i, 0)),
                  pl.BlockSpec((k, tn), lambda i, j: (0, j))],
        out_specs=pl.BlockSpec((tm, tn), lambda i, j: (i, j)),
        out_shape=jax.ShapeDtypeStruct((m, n), BF16),
        compiler_params=_cparams("parallel", "arbitrary"),
        name=name,
    )(a, w)


def _attn_geometry(rows, win_rows):
    rb = ATTN_ROW_BLOCK
    kh = min(win_rows, rows)
    kr = min(rb + kh, rows)
    nrb = rows // rb
    assert rows % rb == 0 and rb >= kh // 2, (rows, rb, kh)
    return rb, kh, kr, nrb


def _window_start(r0, kh, kr, rows):
    return jnp.clip(r0 - kh // 2, 0, rows - kr)


def _attn_bias_table(rpb, rows):
    nh, nr, nc = rpb.shape
    win_rows, win_cols = (nr + 1) // 2, (nc + 1) // 2
    rb, kh, kr, nrb = _attn_geometry(rows, win_rows)
    reps = [0, min(1, nrb - 1), nrb - 1]
    a = np.arange(rb)
    ip = np.arange(kr)
    dr = np.zeros((3, rb, kr), np.int32)
    rvalid = np.zeros((3, rb, kr), bool)
    for kind, rbi in enumerate(reps):
        r0 = rbi * rb
        ws = int(np.clip(r0 - kh // 2, 0, rows - kr))
        r = r0 + a[:, None]
        krow = ws + ip[None, :]
        rs = np.clip(r - kh // 2, 0, rows - kh)
        rvalid[kind] = (krow >= rs) & (krow < rs + kh)
        dr[kind] = np.clip(krow - r + (win_rows - 1), 0, nr - 1)
    cols = np.arange(GRID_W)
    cs = np.clip(cols - win_cols // 2, 0, GRID_W - win_cols)
    kc = cols[None, :]
    cvalid = (kc >= cs[:, None]) & (kc < cs[:, None] + win_cols)
    dc = np.clip(kc - cols[:, None] + (win_cols - 1), 0, nc - 1)
    onehot = (dc[None, :, :] == np.arange(nc)[:, None, None]).astype(np.float32)
    r1 = rpb[:, dr, :]
    bias = jnp.einsum('hkaim,mqc->hkaqic', r1, jnp.asarray(onehot), precision=lax.Precision.HIGHEST)
    valid = rvalid[:, :, None, :, None] & cvalid[None, None, :, None, :]
    bias = jnp.where(jnp.asarray(valid)[None], bias, MASKED)
    return bias.reshape(nh, 3, rb * GRID_W, kr * GRID_W)


def _attn_kernel(q_ref, k_ref, v_ref, bias_ref, o_ref, *, rows, kh, kr, rb, scale):
    r0 = pl.program_id(2) * rb
    start = pl.multiple_of(_window_start(r0, kh, kr, rows) * GRID_W, GRID_W)
    nk = kr * GRID_W
    k = k_ref[pl.ds(start, nk), :]
    v = v_ref[pl.ds(start, nk), :]
    s = lax.dot_general(q_ref[...], k, (((1,), (1,)), ((), ())), preferred_element_type=F32)
    s = s * scale + bias_ref[...]
    p = jnp.exp(s - jnp.max(s, axis=-1, keepdims=True))
    denom = jnp.sum(p, axis=-1, keepdims=True)
    o = jnp.dot(p.astype(v.dtype), v, preferred_element_type=F32)
    o_ref[...] = (o / denom).astype(o_ref.dtype)


def _attention(proj, bias, nb, seq, n_heads, head_dim, aw, win_rows):
    rows = seq // GRID_W
    rb, kh, kr, nrb = _attn_geometry(rows, win_rows)
    tq = rb * GRID_W
    hoff = aw // head_dim

    def kind(r):
        return jnp.where(r == 0, 0, jnp.where(r == nrb - 1, 2, 1))

    return pl.pallas_call(
        functools.partial(_attn_kernel, rows=rows, kh=kh, kr=kr, rb=rb, scale=head_dim ** -0.5),
        grid=(n_heads, nb, nrb),
        in_specs=[pl.BlockSpec((tq, head_dim), lambda h, b, r: (b * nrb + r, h)),
                  pl.BlockSpec((seq, head_dim), lambda h, b, r: (b, hoff + h)),
                  pl.BlockSpec((seq, head_dim), lambda h, b, r: (b, 2 * hoff + h)),
                  pl.BlockSpec((None, None, tq, kr * GRID_W), lambda h, b, r: (h, kind(r), 0, 0))],
        out_specs=pl.BlockSpec((tq, head_dim), lambda h, b, r: (b * nrb + r, h)),
        out_shape=jax.ShapeDtypeStruct((nb * seq, aw), BF16),
        compiler_params=_cparams("parallel", "parallel", "arbitrary"),
        name="na2d_attention",
    )(proj, proj, proj, bias)


def _conv_kernel(cb_ref, cc_ref, ch_ref, ccp_ref, chp_ref, ccn_ref, chn_ref, w_ref, o_ref, *, per_seq):
    i = pl.program_id(0)
    ts = cc_ref.shape[0]
    u = cc_ref[...].astype(F32) * ch_ref[...].astype(F32)
    last_halo = BF16_ROWS - 1
    u_before = ccp_ref[last_halo:, :].astype(F32) * chp_ref[last_halo:, :].astype(F32)
    u_after = ccn_ref[0:1, :].astype(F32) * chn_ref[0:1, :].astype(F32)
    u_before = jnp.where(i % per_seq == 0, 0.0, u_before)
    u_after = jnp.where(i % per_seq == per_seq - 1, 0.0, u_after)
    row = lax.broadcasted_iota(jnp.int32, u.shape, 0)
    u_prev = jnp.where(row == 0, u_before, pltpu.roll(u, 1, axis=0))
    u_next = jnp.where(row == ts - 1, u_after, pltpu.roll(u, ts - 1, axis=0))
    y = w_ref[0:1, :] * u_prev + w_ref[1:2, :] * u + w_ref[2:3, :] * u_next
    o_ref[...] = (cb_ref[...].astype(F32) * y).astype(o_ref.dtype)


def _short_conv(proj, conv_w, seq, col0, cw, ts=512, tc=512):
    t = proj.shape[0]
    ts, tc = _tile(seq, ts), _tile(np.gcd(cw, col0), tc)
    per_seq = seq // ts
    c0 = col0 // tc
    nct = cw // tc
    hb = ts // BF16_ROWS
    last_hb = t // BF16_ROWS - 1

    def main(k):
        return pl.BlockSpec((ts, tc), lambda i, j: (i, c0 + k * nct + j))

    def before(k):
        return pl.BlockSpec((BF16_ROWS, tc), lambda i, j: (jnp.maximum(i * hb - 1, 0), c0 + k * nct + j))

    def after(k):
        return pl.BlockSpec((BF16_ROWS, tc), lambda i, j: (jnp.minimum((i + 1) * hb, last_hb), c0 + k * nct + j))

    return pl.pallas_call(
        functools.partial(_conv_kernel, per_seq=per_seq),
        grid=(t // ts, nct),
        in_specs=[main(0), main(1), main(2), before(1), before(2), after(1), after(2),
                  pl.BlockSpec((conv_w.shape[0], tc), lambda i, j: (0, j))],
        out_specs=pl.BlockSpec((ts, tc), lambda i, j: (i, j)),
        out_shape=jax.ShapeDtypeStruct((t, cw), BF16),
        compiler_params=_cparams("parallel", "parallel"),
        name="short_conv",
    )(proj, proj, proj, proj, proj, proj, proj, conv_w)


def _merge_kernel(a_ref, c_ref, wa_ref, wb_ref, ga_ref, gb_ref, o_ref):
    ya = jnp.dot(a_ref[...], wa_ref[...], preferred_element_type=F32)
    yb = jnp.dot(c_ref[...], wb_ref[...], preferred_element_type=F32)
    ga = jax.nn.sigmoid(ga_ref[...].astype(F32))
    gb = jax.nn.sigmoid(gb_ref[...].astype(F32))
    o_ref[...] = (ga * ya + gb * yb).astype(o_ref.dtype)


def _merge(attn, convm, wa, wb, proj, gate_col0, tm=1024, tn=512):
    t, d = attn.shape[0], wa.shape[1]
    tm, tn = _tile(t, tm), _tile(np.gcd(d, gate_col0), tn)
    g0 = gate_col0 // tn
    nt = d // tn
    return pl.pallas_call(
        _merge_kernel,
        grid=(t // tm, nt),
        in_specs=[pl.BlockSpec((tm, attn.shape[1]), lambda i, j: (i, 0)),
                  pl.BlockSpec((tm, convm.shape[1]), lambda i, j: (i, 0)),
                  pl.BlockSpec((wa.shape[0], tn), lambda i, j: (0, j)),
                  pl.BlockSpec((wb.shape[0], tn), lambda i, j: (0, j)),
                  pl.BlockSpec((tm, tn), lambda i, j: (i, g0 + j)),
                  pl.BlockSpec((tm, tn), lambda i, j: (i, g0 + nt + j))],
        out_specs=pl.BlockSpec((tm, tn), lambda i, j: (i, j)),
        out_shape=jax.ShapeDtypeStruct((t, d), BF16),
        compiler_params=_cparams("parallel", "arbitrary"),
        name="merge_branches",
    )(attn, convm, wa, wb, proj, proj)


def _wo_kernel(m_ref, w_ref, x_ref, mod_ref, o_ref, *, gate_row):
    y = jnp.dot(m_ref[...], w_ref[...], preferred_element_type=F32)
    o_ref[...] = x_ref[...] + mod_ref[gate_row:gate_row + 1, :] * y


def _wo_residual(merged, wo, x2, mod3, gate_row, seq, tm=1024, tn=512):
    t, d = x2.shape
    tm, tn = _tile(seq, tm), _tile(d, tn)
    per_seq = seq // tm
    return pl.pallas_call(
        functools.partial(_wo_kernel, gate_row=gate_row),
        grid=(t // tm, d // tn),
        in_specs=[pl.BlockSpec((tm, merged.shape[1]), lambda i, j: (i, 0)),
                  pl.BlockSpec((wo.shape[0], tn), lambda i, j: (0, j)),
                  pl.BlockSpec((tm, tn), lambda i, j: (i, j)),
                  pl.BlockSpec((None, mod3.shape[1], tn), lambda i, j: (i // per_seq, 0, j))],
        out_specs=pl.BlockSpec((tm, tn), lambda i, j: (i, j)),
        out_shape=jax.ShapeDtypeStruct((t, d), F32),
        compiler_params=_cparams("parallel", "arbitrary"),
        name="wo_residual",
    )(merged, wo, x2, mod3)


def _candidate_pairs():
    k = PEER_TOPK
    return [(p, q) for p in range(k) for q in range(k) if (p + 1) * (q + 1) <= k]


def _top16(s, iota):
    n = s.shape[0]
    rank = jnp.full(s.shape, PEER_TOPK, jnp.int32)
    vals = []
    for it in range(PEER_TOPK):
        m = jnp.max(s, axis=0, keepdims=True)
        idx = jnp.min(jnp.where(s == m, iota, n), axis=0, keepdims=True)
        sel = iota == idx
        rank = jnp.where(sel, it, rank)
        s = jnp.where(sel, -jnp.inf, s)
        vals.append(m)
    return rank, vals


def _peer_select_kernel(q_ref, k1_ref, k2_ref, lim_ref, e1_ref, r2_ref, e2_ref, s2_ref):
    nh, n, dk = k1_ref.shape
    tm = q_ref.shape[0]
    iota = lax.broadcasted_iota(jnp.int32, (n, tm), 0)
    nt = (((1,), (1,)), ((), ()))
    a_rows, b_rows = [], []
    for h in range(nh):
        q1 = q_ref[:, (2 * h) * dk:(2 * h + 1) * dk]
        q2 = q_ref[:, (2 * h + 1) * dk:(2 * h + 2) * dk]
        s1 = lax.dot_general(k1_ref[h], q1, nt, preferred_element_type=F32)
        s2 = lax.dot_general(k2_ref[h], q2, nt, preferred_element_type=F32)
        rank1, a = _top16(s1, iota)
        rank2, b = _top16(s2, iota)
        e1_ref[h] = s1
        s2_ref[h] = s2
        lim_ref[h] = rank1.astype(F32)
        r2_ref[h] = rank2.astype(F32).astype(r2_ref.dtype)
        a_rows.append(a)
        b_rows.append(b)
    big_a = [jnp.concatenate([a_rows[h][p] for h in range(nh)], axis=0) for p in range(PEER_TOPK)]
    big_b = [jnp.concatenate([b_rows[h][q] for h in range(nh)], axis=0) for q in range(PEER_TOPK)]
    pairs = _candidate_pairs()
    val = [big_a[p] + big_b[q] for p, q in pairs]
    beaten = []
    for c, (p, q) in enumerate(pairs):
        beaten.append(jnp.full((nh, tm), float((p + 1) * (q + 1) - 1), F32))
    for c1, (p1, q1) in enumerate(pairs):
        for c2 in range(c1 + 1, len(pairs)):
            p2, q2 = pairs[c2]
            if (p1 <= p2 and q1 <= q2) or (p2 <= p1 and q2 <= q1):
                continue
            first_wins = jnp.where(val[c1] >= val[c2], 1.0, 0.0)
            beaten[c2] = beaten[c2] + first_wins
            beaten[c1] = beaten[c1] + (1.0 - first_wins)
    count = [jnp.zeros((nh, tm), F32) for _ in range(PEER_TOPK)]
    z = jnp.zeros((nh, tm), F32)
    for c, (p, q) in enumerate(pairs):
        chosen = beaten[c] < float(PEER_TOPK)
        count[p] = count[p] + jnp.where(chosen, 1.0, 0.0)
        z = z + jnp.where(chosen, jnp.exp((big_a[p] - big_a[0]) + (big_b[q] - big_b[0])), 0.0)
    inv_z = 1.0 / z
    for h in range(nh):
        rank1 = lim_ref[h]
        lim = jnp.zeros((n, tm), F32)
        for p in range(PEER_TOPK):
            lim = jnp.where(rank1 == float(p), count[p][h:h + 1, :], lim)
        lim_ref[h] = lim
        e1_ref[h] = jnp.exp(e1_ref[h] - big_a[0][h:h + 1, :])
        e2_ref[h] = (jnp.exp(s2_ref[h] - big_b[0][h:h + 1, :]) * inv_z[h:h + 1, :]).astype(e2_ref.dtype)


def _peer_select(qp, k1, k2, tm=256):
    t = qp.shape[0]
    nh, n, dk = k1.shape
    out = jax.ShapeDtypeStruct((nh, n, t), F32)
    out16 = jax.ShapeDtypeStruct((nh, n, t), BF16)
    ospec = pl.BlockSpec((nh, n, tm), lambda i: (0, 0, i))
    return pl.pallas_call(
        _peer_select_kernel,
        grid=(t // tm,),
        in_specs=[pl.BlockSpec((tm, qp.shape[1]), lambda i: (i, 0)),
                  pl.BlockSpec((nh, n, dk), lambda i: (0, 0, 0)),
                  pl.BlockSpec((nh, n, dk), lambda i: (0, 0, 0))],
        out_specs=[ospec, ospec, ospec, ospec],
        out_shape=[out, out, out16, out16],
        scratch_shapes=[pltpu.VMEM((nh, n, tm), F32)],
        compiler_params=_cparams("parallel"),
        name="peer_select",
    )(qp, k1, k2)


F32_ROWS = 8
PEER_SUBTILE = 512


def _peer_dense_kernel(h_ref, u_ref, vt_ref, lim_ref, e1_ref, r2_ref, e2_ref, o_ref):
    te = u_ref.shape[0]
    nh, n, _ = r2_ref.shape
    per_sub = PEER_SUBTILE // n

    @pl.when(pl.program_id(1) == 0)
    def _():
        o_ref[...] = jnp.zeros_like(o_ref)

    def gate(c):
        g = None
        for h in range(nh):
            lim = lim_ref[h, c:c + 1, :].astype(BF16)
            e1 = e1_ref[h, c:c + 1, :].astype(BF16)
            term = jnp.where(r2_ref[h] < lim, e2_ref[h] * e1, jnp.zeros((), BF16))
            g = term if g is None else g + term
        return g

    hh = h_ref[...]
    subs = [slice(s * PEER_SUBTILE, (s + 1) * PEER_SUBTILE) for s in range(te // PEER_SUBTILE)]
    acts = [jnp.dot(u_ref[rows, :], hh, preferred_element_type=F32) for rows in subs]
    for s, rows in enumerate(subs):
        g = jnp.concatenate([gate(s * per_sub + c) for c in range(per_sub)], axis=0)
        ga = g * jax.nn.gelu(acts[s]).astype(BF16)
        o_ref[...] += jnp.dot(vt_ref[:, rows], ga, preferred_element_type=F32)


def _peer_dense(h2t, u, vt, lim, e1, r2, e2, tm=512):
    d, t = h2t.shape
    ne = u.shape[0]
    nh, n, _ = lim.shape
    tm = _tile(t, tm)
    te = F32_ROWS * n
    assert te % PEER_SUBTILE == 0 and PEER_SUBTILE % n == 0
    once = pl.Buffered(1)
    rows = pl.BlockSpec((nh, F32_ROWS, tm), lambda i, e: (0, e, i))
    full = pl.BlockSpec((nh, n, tm), lambda i, e: (0, 0, i))
    return pl.pallas_call(
        _peer_dense_kernel,
        grid=(t // tm, ne // te),
        in_specs=[pl.BlockSpec((d, tm), lambda i, e: (0, i), pipeline_mode=once),
                  pl.BlockSpec((te, d), lambda i, e: (e, 0)),
                  pl.BlockSpec((d, te), lambda i, e: (0, e)),
                  rows, rows, full, full],
        out_specs=pl.BlockSpec((d, tm), lambda i, e: (0, i), pipeline_mode=once),
        out_shape=jax.ShapeDtypeStruct((d, t), F32),
        compiler_params=_cparams("parallel", "arbitrary"),
        name="peer_dense",
    )(h2t, u, vt, lim, e1, r2, e2)


def _peer_residual_kernel(x_ref, pt_ref, mod_ref, g_ref, o_ref, *, gate_row, final_norm):
    y = x_ref[...] + mod_ref[gate_row:gate_row + 1, :] * pt_ref[...].T
    if final_norm:
        y = y * lax.rsqrt(jnp.mean(y * y, axis=-1, keepdims=True) + RMS_EPS) * g_ref[...]
    o_ref[...] = y


def _peer_residual(x1, peer_t, mod3, gate_row, g, seq, final_norm, tm=256):
    t, d = x1.shape
    per_seq = seq // tm
    return pl.pallas_call(
        functools.partial(_peer_residual_kernel, gate_row=gate_row, final_norm=final_norm),
        grid=(t // tm,),
        in_specs=[pl.BlockSpec((tm, d), lambda i: (i, 0)),
                  pl.BlockSpec((d, tm), lambda i: (0, i)),
                  pl.BlockSpec((None, mod3.shape[1], d), lambda i: (i // per_seq, 0, 0)),
                  pl.BlockSpec((1, d), lambda i: (0, 0))],
        out_specs=pl.BlockSpec((tm, d), lambda i: (i, 0)),
        out_shape=jax.ShapeDtypeStruct((t, d), F32),
        compiler_params=_cparams("parallel"),
        name="peer_residual",
    )(x1, peer_t, mod3, g.reshape(1, d))


def kernel(x, c, w_ada, b_ada, norm1_g, w_in, rpb, conv_w, w_attn_out, w_conv_out, w_o, norm2_g,
           w_q_peer, sub_keys_1, sub_keys_2, expert_u, expert_v, norm_f_g):
    nb, seq, d = x.shape
    depth = w_ada.shape[0]
    n_mod = w_ada.shape[2] // d
    aw = w_attn_out.shape[1]
    cw = w_conv_out.shape[1]
    n_heads = rpb.shape[1]
    head_dim = aw // n_heads
    rows = seq // GRID_W
    x2 = x.reshape(nb * seq, d)
    for l in range(depth):
        mod3 = _mod(c, w_ada[l], b_ada[l]).reshape(nb, n_mod, d)

        h1 = _norm_mod(x2, norm1_g[l], mod3, 0, seq)
        proj = _matmul(h1, w_in[l].astype(BF16), 1024, 1024, "in_proj")
        bias = _attn_bias_table(rpb[l], rows)
        attn = _attention(proj, bias, nb, seq, n_heads, head_dim, aw, (rpb.shape[2] + 1) // 2)
        convm = _short_conv(proj, conv_w[l], seq, 3 * aw, cw)
        merged = _merge(attn, convm, w_attn_out[l].astype(BF16), w_conv_out[l].astype(BF16),
                        proj, 3 * aw + 3 * cw)
        x2 = _wo_residual(merged, w_o[l].astype(BF16), x2, mod3, 2, seq)

        h2, h2t = _norm_mod(x2, norm2_g[l], mod3, 3, seq, with_transposed=True)
        qp = _matmul(h2, w_q_peer[l].astype(BF16), 1024, 1024, "peer_query")
        lim, e1, r2, e2 = _peer_select(qp, sub_keys_1[l].astype(BF16), sub_keys_2[l].astype(BF16))
        peer_t = _peer_dense(h2t, expert_u[l].astype(BF16), expert_v[l].T.astype(BF16), lim, e1, r2, e2)
        x2 = _peer_residual(x2, peer_t, mod3, 5, norm_f_g, seq, final_norm=(l == depth - 1))
    return x2.reshape(nb, seq, d)
```

```python
import functools

import numpy as np
import jax
import jax.numpy as jnp
from jax import lax
from jax.experimental import pallas as pl
from jax.experimental.pallas import tpu as pltpu

GRID_W = 64
RMS_EPS = 1e-6
PEER_TOPK = 16
LANES = 128
BF16_ROWS = 16
V7X_VMEM_BYTES = 64 * 1024 * 1024
VMEM_LIMIT = V7X_VMEM_BYTES - 8 * 1024 * 1024
ATTN_ROW_BLOCK = 8
ATTN_GROUP_ROWS = 4
MASKED = -1e30

F32 = jnp.float32
BF16 = jnp.bfloat16


def _cparams(*sem):
    return pltpu.CompilerParams(dimension_semantics=sem, vmem_limit_bytes=VMEM_LIMIT)


def _tile(n, target, align=LANES):
    best = None
    for cand in range(align, min(n, target) + 1, align):
        if n % cand == 0:
            best = cand
    assert best is not None, (n, target, align)
    return best


def _mod_kernel(cb_ref, w_ref, b_ref, o_ref):
    nb = cb_ref.shape[0]
    tn = w_ref.shape[1]
    for j in range(tn // LANES):
        cols = slice(j * LANES, (j + 1) * LANES)
        w = w_ref[:, cols]
        for b in range(nb):
            o_ref[b:b + 1, cols] = jnp.sum(w * cb_ref[b], axis=0, keepdims=True) + b_ref[:, cols]


def _mod(c, w_ada, b_ada, tn=512):
    nb, d = c.shape
    n = w_ada.shape[1]
    cb = jnp.broadcast_to(c[:, :, None], (nb, d, LANES))
    return pl.pallas_call(
        _mod_kernel,
        grid=(n // tn,),
        in_specs=[pl.BlockSpec((nb, d, LANES), lambda j: (0, 0, 0)),
                  pl.BlockSpec((d, tn), lambda j: (0, j)),
                  pl.BlockSpec((1, tn), lambda j: (0, j))],
        out_specs=pl.BlockSpec((nb, tn), lambda j: (0, j)),
        out_shape=jax.ShapeDtypeStruct((nb, n), F32),
        compiler_params=_cparams("arbitrary"),
        name="ada_mod",
    )(cb, w_ada, b_ada.reshape(1, n))


def _norm_mod_kernel(x_ref, g_ref, mod_ref, o_ref, *ot_ref, shift_row):
    x = x_ref[...]
    y = x * lax.rsqrt(jnp.mean(x * x, axis=-1, keepdims=True) + RMS_EPS) * g_ref[...]
    shift = mod_ref[shift_row:shift_row + 1, :]
    scale = mod_ref[shift_row + 1:shift_row + 2, :]
    h = y * (1.0 + scale) + shift
    o_ref[...] = h.astype(o_ref.dtype)
    if ot_ref:
        ot_ref[0][...] = h.T.astype(o_ref.dtype)


def _norm_mod(x2, g, mod3, shift_row, seq, with_transposed=False, tm=256):
    t, d = x2.shape
    per_seq = seq // tm
    out_specs = [pl.BlockSpec((tm, d), lambda i: (i, 0))]
    out_shape = [jax.ShapeDtypeStruct((t, d), BF16)]
    if with_transposed:
        out_specs.append(pl.BlockSpec((d, tm), lambda i: (0, i)))
        out_shape.append(jax.ShapeDtypeStruct((d, t), BF16))
    out = pl.pallas_call(
        functools.partial(_norm_mod_kernel, shift_row=shift_row),
        grid=(t // tm,),
        in_specs=[pl.BlockSpec((tm, d), lambda i: (i, 0)),
                  pl.BlockSpec((1, d), lambda i: (0, 0)),
                  pl.BlockSpec((None, mod3.shape[1], d), lambda i: (i // per_seq, 0, 0))],
        out_specs=out_specs,
        out_shape=out_shape,
        compiler_params=_cparams("parallel"),
        name="norm_mod",
    )(x2, g.reshape(1, d), mod3)
    return out if with_transposed else out[0]


def _mm_kernel(a_ref, w_ref, o_ref):
    o_ref[...] = jnp.dot(a_ref[...], w_ref[...], preferred_element_type=F32).astype(o_ref.dtype)


def _matmul(a, w, tm, tn, name):
    m, k = a.shape
    n = w.shape[1]
    tm, tn = _tile(m, tm), _tile(n, tn)
    return pl.pallas_call(
        _mm_kernel,
        grid=(m // tm, n // tn),
        in_specs=[pl.BlockSpec((tm, k), lambda i, j: (i, 0)),
                  pl.BlockSpec((k, tn), lambda i, j: (0, j))],
        out_specs=pl.BlockSpec((tm, tn), lambda i, j: (i, j)),
        out_shape=jax.ShapeDtypeStruct((m, n), BF16),
        compiler_params=_cparams("parallel", "arbitrary"),
        name=name,
    )(a, w)


def _attn_geometry(rows, win_rows):
    ga = ATTN_GROUP_ROWS
    kh = min(win_rows, rows)
    gkr = min(ga + kh, rows)
    assert rows % ATTN_ROW_BLOCK == 0 and ATTN_ROW_BLOCK % ga == 0 and ga >= kh // 2, (rows, ga, kh)
    return ga, kh, gkr


def _attn_col_bias(rpb):
    nh, nr, nc = rpb.shape
    win_cols = (nc + 1) // 2
    cols = np.arange(GRID_W)
    cs = np.clip(cols - win_cols // 2, 0, GRID_W - win_cols)
    kc = cols[None, :]
    cvalid = (kc >= cs[:, None]) & (kc < cs[:, None] + win_cols)
    dc = np.clip(kc - cols[:, None] + (win_cols - 1), 0, nc - 1)
    onehot = (dc[None] == np.arange(nc)[:, None, None]).astype(np.float32)
    cb = jnp.einsum('hrm,mqc->hrqc', rpb, jnp.asarray(onehot), precision=lax.Precision.HIGHEST)
    return jnp.where(jnp.asarray(cvalid), cb, MASKED)


def _attn_kernel(q_ref, k_ref, v_ref, cb_ref, o_ref, bias_ref, *, rows, win_rows, scale):
    ga, kh, gkr = _attn_geometry(rows, win_rows)
    gq, gk = ga * GRID_W, gkr * GRID_W
    first_rows = (0, min(ga, rows - ga), rows - ga)

    @pl.when((pl.program_id(1) == 0) & (pl.program_id(2) == 0))
    def _():
        for kind, first in enumerate(first_rows):
            ws = int(np.clip(first - kh // 2, 0, rows - gkr))
            for a in range(ga):
                r = first + a
                rs = int(np.clip(r - kh // 2, 0, rows - kh))
                for i in range(gkr):
                    krow = ws + i
                    dst = (kind, slice(a * GRID_W, (a + 1) * GRID_W), slice(i * GRID_W, (i + 1) * GRID_W))
                    if rs <= krow < rs + kh:
                        bias_ref[dst] = cb_ref[krow - r + (win_rows - 1)]
                    else:
                        bias_ref[dst] = jnp.full((GRID_W, GRID_W), MASKED, F32)

    r0 = pl.program_id(2) * ATTN_ROW_BLOCK
    groups = []
    for g in range(ATTN_ROW_BLOCK // ga):
        first = r0 + g * ga
        start = pl.multiple_of(jnp.clip(first - kh // 2, 0, rows - gkr) * GRID_W, GRID_W)
        kind = jnp.where(first == 0, 0, jnp.where(first == rows - ga, 2, 1))
        k = k_ref[pl.ds(start, gk), :]
        s = lax.dot_general(q_ref[g * gq:(g + 1) * gq, :], k, (((1,), (1,)), ((), ())),
                            preferred_element_type=F32)
        groups.append((start, kind, s))
    for g, (start, kind, s) in enumerate(groups):
        s = s * scale + bias_ref[kind]
        p = jnp.exp(s - jnp.max(s, axis=-1, keepdims=True))
        denom = jnp.sum(p, axis=-1, keepdims=True)
        o = jnp.dot(p.astype(v_ref.dtype), v_ref[pl.ds(start, gk), :], preferred_element_type=F32)
        o_ref[g * gq:(g + 1) * gq, :] = (o / denom).astype(o_ref.dtype)


def _attention(proj, col_bias, nb, seq, n_heads, head_dim, aw):
    rows = seq // GRID_W
    nr = col_bias.shape[1]
    win_rows = (nr + 1) // 2
    ga, kh, gkr = _attn_geometry(rows, win_rows)
    nrb = rows // ATTN_ROW_BLOCK
    tq = ATTN_ROW_BLOCK * GRID_W
    hoff = aw // head_dim
    return pl.pallas_call(
        functools.partial(_attn_kernel, rows=rows, win_rows=win_rows, scale=head_dim ** -0.5),
        grid=(n_heads, nb, nrb),
        in_specs=[pl.BlockSpec((tq, head_dim), lambda h, b, r: (b * nrb + r, h)),
                  pl.BlockSpec((seq, head_dim), lambda h, b, r: (b, hoff + h)),
                  pl.BlockSpec((seq, head_dim), lambda h, b, r: (b, 2 * hoff + h)),
                  pl.BlockSpec((None, nr, GRID_W, GRID_W), lambda h, b, r: (h, 0, 0, 0))],
        out_specs=pl.BlockSpec((tq, head_dim), lambda h, b, r: (b * nrb + r, h)),
        out_shape=jax.ShapeDtypeStruct((nb * seq, aw), BF16),
        scratch_shapes=[pltpu.VMEM((3, ga * GRID_W, gkr * GRID_W), F32)],
        compiler_params=_cparams("arbitrary", "arbitrary", "arbitrary"),
        name="na2d_attention",
    )(proj, proj, proj, col_bias)


def _conv_kernel(cb_ref, cc_ref, ch_ref, ccp_ref, chp_ref, ccn_ref, chn_ref, w_ref, o_ref, *, per_seq):
    i = pl.program_id(0)
    ts = cc_ref.shape[0]
    u = cc_ref[...].astype(F32) * ch_ref[...].astype(F32)
    last_halo = BF16_ROWS - 1
    u_before = ccp_ref[last_halo:, :].astype(F32) * chp_ref[last_halo:, :].astype(F32)
    u_after = ccn_ref[0:1, :].astype(F32) * chn_ref[0:1, :].astype(F32)
    u_before = jnp.where(i % per_seq == 0, 0.0, u_before)
    u_after = jnp.where(i % per_seq == per_seq - 1, 0.0, u_after)
    row = lax.broadcasted_iota(jnp.int32, u.shape, 0)
    u_prev = jnp.where(row == 0, u_before, pltpu.roll(u, 1, axis=0))
    u_next = jnp.where(row == ts - 1, u_after, pltpu.roll(u, ts - 1, axis=0))
    y = w_ref[0:1, :] * u_prev + w_ref[1:2, :] * u + w_ref[2:3, :] * u_next
    o_ref[...] = (cb_ref[...].astype(F32) * y).astype(o_ref.dtype)


def _short_conv(proj, conv_w, seq, col0, cw, ts=512, tc=512):
    t = proj.shape[0]
    ts, tc = _tile(seq, ts), _tile(np.gcd(cw, col0), tc)
    per_seq = seq // ts
    c0 = col0 // tc
    nct = cw // tc
    hb = ts // BF16_ROWS
    last_hb = t // BF16_ROWS - 1

    def main(k):
        return pl.BlockSpec((ts, tc), lambda i, j: (i, c0 + k * nct + j))

    def before(k):
        return pl.BlockSpec((BF16_ROWS, tc), lambda i, j: (jnp.maximum(i * hb - 1, 0), c0 + k * nct + j))

    def after(k):
        return pl.BlockSpec((BF16_ROWS, tc), lambda i, j: (jnp.minimum((i + 1) * hb, last_hb), c0 + k * nct + j))

    return pl.pallas_call(
        functools.partial(_conv_kernel, per_seq=per_seq),
        grid=(t // ts, nct),
        in_specs=[main(0), main(1), main(2), before(1), before(2), after(1), after(2),
                  pl.BlockSpec((conv_w.shape[0], tc), lambda i, j: (0, j))],
        out_specs=pl.BlockSpec((ts, tc), lambda i, j: (i, j)),
        out_shape=jax.ShapeDtypeStruct((t, cw), BF16),
        compiler_params=_cparams("parallel", "parallel"),
        name="short_conv",
    )(proj, proj, proj, proj, proj, proj, proj, conv_w)


def _merge_kernel(a_ref, c_ref, wa_ref, wb_ref, ga_ref, gb_ref, o_ref):
    ya = jnp.dot(a_ref[...], wa_ref[...], preferred_element_type=F32)
    yb = jnp.dot(c_ref[...], wb_ref[...], preferred_element_type=F32)
    ga = jax.nn.sigmoid(ga_ref[...].astype(F32))
    gb = jax.nn.sigmoid(gb_ref[...].astype(F32))
    o_ref[...] = (ga * ya + gb * yb).astype(o_ref.dtype)


def _merge(attn, convm, wa, wb, proj, gate_col0, tm=1024, tn=512):
    t, d = attn.shape[0], wa.shape[1]
    tm, tn = _tile(t, tm), _tile(np.gcd(d, gate_col0), tn)
    g0 = gate_col0 // tn
    nt = d // tn
    return pl.pallas_call(
        _merge_kernel,
        grid=(t // tm, nt),
        in_specs=[pl.BlockSpec((tm, attn.shape[1]), lambda i, j: (i, 0)),
                  pl.BlockSpec((tm, convm.shape[1]), lambda i, j: (i, 0)),
                  pl.BlockSpec((wa.shape[0], tn), lambda i, j: (0, j)),
                  pl.BlockSpec((wb.shape[0], tn), lambda i, j: (0, j)),
                  pl.BlockSpec((tm, tn), lambda i, j: (i, g0 + j)),
                  pl.BlockSpec((tm, tn), lambda i, j: (i, g0 + nt + j))],
        out_specs=pl.BlockSpec((tm, tn), lambda i, j: (i, j)),
        out_shape=jax.ShapeDtypeStruct((t, d), BF16),
        compiler_params=_cparams("parallel", "arbitrary"),
        name="merge_branches",
    )(attn, convm, wa, wb, proj, proj)


def _wo_kernel(m_ref, w_ref, x_ref, mod_ref, o_ref, *, gate_row):
    y = jnp.dot(m_ref[...], w_ref[...], preferred_element_type=F32)
    o_ref[...] = x_ref[...] + mod_ref[gate_row:gate_row + 1, :] * y


def _wo_residual(merged, wo, x2, mod3, gate_row, seq, tm=1024, tn=1024):
    t, d = x2.shape
    tm, tn = _tile(seq, tm), _tile(d, tn)
    per_seq = seq // tm
    return pl.pallas_call(
        functools.partial(_wo_kernel, gate_row=gate_row),
        grid=(t // tm, d // tn),
        in_specs=[pl.BlockSpec((tm, merged.shape[1]), lambda i, j: (i, 0)),
                  pl.BlockSpec((wo.shape[0], tn), lambda i, j: (0, j)),
                  pl.BlockSpec((tm, tn), lambda i, j: (i, j)),
                  pl.BlockSpec((None, mod3.shape[1], tn), lambda i, j: (i // per_seq, 0, j))],
        out_specs=pl.BlockSpec((tm, tn), lambda i, j: (i, j)),
        out_shape=jax.ShapeDtypeStruct((t, d), F32),
        compiler_params=_cparams("parallel", "arbitrary"),
        name="wo_residual",
    )(merged, wo, x2, mod3)


def _candidate_pairs():
    k = PEER_TOPK
    return [(p, q) for p in range(k) for q in range(k) if (p + 1) * (q + 1) <= k]


_RANK_CODE = 2.0 ** 100


def _top16_exact(s, iota):
    n = s.shape[0]
    rank = jnp.full(s.shape, float(PEER_TOPK), F32)
    vals = []
    for it in range(PEER_TOPK):
        m = jnp.max(s, axis=0, keepdims=True)
        idx = jnp.min(jnp.where(s == m, iota, float(n)), axis=0, keepdims=True)
        sel = iota == idx
        rank = jnp.where(sel, float(it), rank)
        s = jnp.where(sel, -jnp.inf, s)
        vals.append(m)
    return rank, vals


def _top16_fast(s):
    vals = []
    for it in range(PEER_TOPK):
        m = jnp.max(s, axis=0, keepdims=True)
        s = jnp.where(s == m, -_RANK_CODE * (1.0 + it / PEER_TOPK), s)
        vals.append(m)
    is_taken = s <= -_RANK_CODE
    rank = jnp.where(is_taken, (s * (-1.0 / _RANK_CODE) - 1.0) * PEER_TOPK, float(PEER_TOPK))
    taken = jnp.sum(jnp.where(is_taken, 1.0, 0.0), axis=0, keepdims=True)
    return rank, vals, taken


def _peer_select_kernel(q_ref, k1_ref, k2_ref, lim_ref, e1_ref, r2_ref, e2_ref, s1_ref, s2_ref, a_ref, b_ref):
    nh, n, dk = k1_ref.shape
    tm = q_ref.shape[0]
    nt = (((1,), (1,)), ((), ()))
    most_taken = jnp.zeros((1, tm), F32)
    for h in range(nh):
        q1 = q_ref[:, (2 * h) * dk:(2 * h + 1) * dk]
        q2 = q_ref[:, (2 * h + 1) * dk:(2 * h + 2) * dk]
        s1 = lax.dot_general(k1_ref[h], q1, nt, preferred_element_type=F32)
        s2 = lax.dot_general(k2_ref[h], q2, nt, preferred_element_type=F32)
        s1_ref[h] = s1
        s2_ref[h] = s2
        rank1, a, taken1 = _top16_fast(s1)
        rank2, b, taken2 = _top16_fast(s2)
        lim_ref[h] = rank1
        r2_ref[h] = rank2.astype(r2_ref.dtype)
        for p in range(PEER_TOPK):
            a_ref[p * nh + h:p * nh + h + 1, :] = a[p]
            b_ref[p * nh + h:p * nh + h + 1, :] = b[p]
        most_taken = jnp.maximum(most_taken, jnp.maximum(taken1, taken2))

    @pl.when(jnp.max(most_taken) > float(PEER_TOPK))
    def _():
        iota = lax.broadcasted_iota(jnp.int32, (n, tm), 0).astype(F32)

        def redo(h, carry):
            rank1, a = _top16_exact(s1_ref[h], iota)
            rank2, b = _top16_exact(s2_ref[h], iota)
            lim_ref[h] = rank1
            r2_ref[h] = rank2.astype(r2_ref.dtype)
            for p in range(PEER_TOPK):
                a_ref[pl.ds(p * nh + h, 1), :] = a[p]
                b_ref[pl.ds(p * nh + h, 1), :] = b[p]
            return carry

        lax.fori_loop(0, nh, redo, 0)

    big_a = [a_ref[p * nh:(p + 1) * nh, :] for p in range(PEER_TOPK)]
    big_b = [b_ref[q * nh:(q + 1) * nh, :] for q in range(PEER_TOPK)]
    pairs = _candidate_pairs()
    val = [big_a[p] + big_b[q] for p, q in pairs]
    beaten = []
    for c, (p, q) in enumerate(pairs):
        beaten.append(jnp.full((nh, tm), float((p + 1) * (q + 1) - 1), F32))
    for c1, (p1, q1) in enumerate(pairs):
        for c2 in range(c1 + 1, len(pairs)):
            p2, q2 = pairs[c2]
            if (p1 <= p2 and q1 <= q2) or (p2 <= p1 and q2 <= q1):
                continue
            first_wins = jnp.where(val[c1] >= val[c2], 1.0, 0.0)
            beaten[c2] = beaten[c2] + first_wins
            beaten[c1] = beaten[c1] + (1.0 - first_wins)
    count = [jnp.zeros((nh, tm), F32) for _ in range(PEER_TOPK)]
    z = jnp.zeros((nh, tm), F32)
    for c, (p, q) in enumerate(pairs):
        chosen = beaten[c] < float(PEER_TOPK)
        count[p] = count[p] + jnp.where(chosen, 1.0, 0.0)
        z = z + jnp.where(chosen, jnp.exp((big_a[p] - big_a[0]) + (big_b[q] - big_b[0])), 0.0)
    inv_z = 1.0 / z
    for h in range(nh):
        rank1 = lim_ref[h]
        lim = jnp.zeros((n, tm), F32)
        for p in range(PEER_TOPK):
            lim = jnp.where(rank1 == float(p), count[p][h:h + 1, :], lim)
        lim_ref[h] = lim
        e1_ref[h] = jnp.exp(s1_ref[h] - big_a[0][h:h + 1, :])
        e2_ref[h] = (jnp.exp(s2_ref[h] - big_b[0][h:h + 1, :]) * inv_z[h:h + 1, :]).astype(e2_ref.dtype)


def _peer_select(qp, k1, k2, tm=256):
    t = qp.shape[0]
    nh, n, dk = k1.shape
    out = jax.ShapeDtypeStruct((nh, n, t), F32)
    out16 = jax.ShapeDtypeStruct((nh, n, t), BF16)
    ospec = pl.BlockSpec((nh, n, tm), lambda i: (0, 0, i))
    return pl.pallas_call(
        _peer_select_kernel,
        grid=(t // tm,),
        in_specs=[pl.BlockSpec((tm, qp.shape[1]), lambda i: (i, 0)),
                  pl.BlockSpec((nh, n, dk), lambda i: (0, 0, 0)),
                  pl.BlockSpec((nh, n, dk), lambda i: (0, 0, 0))],
        out_specs=[ospec, ospec, ospec, ospec],
        out_shape=[out, out, out16, out16],
        scratch_shapes=[pltpu.VMEM((nh, n, tm), F32), pltpu.VMEM((nh, n, tm), F32),
                        pltpu.VMEM((PEER_TOPK * nh, tm), F32), pltpu.VMEM((PEER_TOPK * nh, tm), F32)],
        compiler_params=_cparams("parallel"),
        name="peer_select",
    )(qp, k1, k2)


F32_ROWS = 8
PEER_SUBTILE = 512


def _peer_dense_kernel(h_ref, u_ref, vt_ref, lim_ref, e1_ref, r2_ref, e2_ref, o_ref):
    te = u_ref.shape[0]
    nh, n, _ = r2_ref.shape
    per_sub = PEER_SUBTILE // n

    @pl.when(pl.program_id(1) == 0)
    def _():
        o_ref[...] = jnp.zeros_like(o_ref)

    def gate(c):
        g = None
        for h in range(nh):
            lim = lim_ref[h, c:c + 1, :].astype(BF16)
            e1 = e1_ref[h, c:c + 1, :].astype(BF16)
            term = jnp.where(r2_ref[h] < lim, e2_ref[h] * e1, jnp.zeros((), BF16))
            g = term if g is None else g + term
        return g

    hh = h_ref[...]
    subs = [slice(s * PEER_SUBTILE, (s + 1) * PEER_SUBTILE) for s in range(te // PEER_SUBTILE)]
    acts = [jnp.dot(u_ref[rows, :], hh, preferred_element_type=F32) for rows in subs]
    for s, rows in enumerate(subs):
        g = jnp.concatenate([gate(s * per_sub + c) for c in range(per_sub)], axis=0)
        ga = g * jax.nn.gelu(acts[s]).astype(BF16)
        o_ref[...] += jnp.dot(vt_ref[:, rows], ga, preferred_element_type=F32)


def _peer_dense(h2t, u, vt, lim, e1, r2, e2, tm=512):
    d, t = h2t.shape
    ne = u.shape[0]
    nh, n, _ = lim.shape
    tm = _tile(t, tm)
    te = F32_ROWS * n
    assert te % PEER_SUBTILE == 0 and PEER_SUBTILE % n == 0
    once = pl.Buffered(1)
    rows = pl.BlockSpec((nh, F32_ROWS, tm), lambda i, e: (0, e, i))
    full = pl.BlockSpec((nh, n, tm), lambda i, e: (0, 0, i))
    return pl.pallas_call(
        _peer_dense_kernel,
        grid=(t // tm, ne // te),
        in_specs=[pl.BlockSpec((d, tm), lambda i, e: (0, i), pipeline_mode=once),
                  pl.BlockSpec((te, d), lambda i, e: (e, 0)),
                  pl.BlockSpec((d, te), lambda i, e: (0, e)),
                  rows, rows, full, full],
        out_specs=pl.BlockSpec((d, tm), lambda i, e: (0, i), pipeline_mode=once),
        out_shape=jax.ShapeDtypeStruct((d, t), F32),
        compiler_params=_cparams("parallel", "arbitrary"),
        name="peer_dense",
    )(h2t, u, vt, lim, e1, r2, e2)


def _peer_residual_kernel(x_ref, pt_ref, mod_ref, g_ref, o_ref, *, gate_row, final_norm):
    y = x_ref[...] + mod_ref[gate_row:gate_row + 1, :] * pt_ref[...].T
    if final_norm:
        y = y * lax.rsqrt(jnp.mean(y * y, axis=-1, keepdims=True) + RMS_EPS) * g_ref[...]
    o_ref[...] = y


def _peer_residual(x1, peer_t, mod3, gate_row, g, seq, final_norm, tm=256):
    t, d = x1.shape
    per_seq = seq // tm
    return pl.pallas_call(
        functools.partial(_peer_residual_kernel, gate_row=gate_row, final_norm=final_norm),
        grid=(t // tm,),
        in_specs=[pl.BlockSpec((tm, d), lambda i: (i, 0)),
                  pl.BlockSpec((d, tm), lambda i: (0, i)),
                  pl.BlockSpec((None, mod3.shape[1], d), lambda i: (i // per_seq, 0, 0)),
                  pl.BlockSpec((1, d), lambda i: (0, 0))],
        out_specs=pl.BlockSpec((tm, d), lambda i: (i, 0)),
        out_shape=jax.ShapeDtypeStruct((t, d), F32),
        compiler_params=_cparams("parallel"),
        name="peer_residual",
    )(x1, peer_t, mod3, g.reshape(1, d))


def kernel(x, c, w_ada, b_ada, norm1_g, w_in, rpb, conv_w, w_attn_out, w_conv_out, w_o, norm2_g,
           w_q_peer, sub_keys_1, sub_keys_2, expert_u, expert_v, norm_f_g):
    nb, seq, d = x.shape
    depth = w_ada.shape[0]
    n_mod = w_ada.shape[2] // d
    aw = w_attn_out.shape[1]
    cw = w_conv_out.shape[1]
    n_heads = rpb.shape[1]
    head_dim = aw // n_heads
    rows = seq // GRID_W
    x2 = x.reshape(nb * seq, d)
    for l in range(depth):
        mod3 = _mod(c, w_ada[l], b_ada[l]).reshape(nb, n_mod, d)

        h1 = _norm_mod(x2, norm1_g[l], mod3, 0, seq)
        proj = _matmul(h1, w_in[l].astype(BF16), 1024, 1024, "in_proj")
        attn = _attention(proj, _attn_col_bias(rpb[l]), nb, seq, n_heads, head_dim, aw)
        convm = _short_conv(proj, conv_w[l], seq, 3 * aw, cw)
        merged = _merge(attn, convm, w_attn_out[l].astype(BF16), w_conv_out[l].astype(BF16),
                        proj, 3 * aw + 3 * cw)
        x2 = _wo_residual(merged, w_o[l].astype(BF16), x2, mod3, 2, seq)

        h2, h2t = _norm_mod(x2, norm2_g[l], mod3, 3, seq, with_transposed=True)
        qp = _matmul(h2, w_q_peer[l].astype(BF16), 1024, 1024, "peer_query")
        lim, e1, r2, e2 = _peer_select(qp, sub_keys_1[l].astype(BF16), sub_keys_2[l].astype(BF16))
        peer_t = _peer_dense(h2t, expert_u[l].astype(BF16), expert_v[l].T.astype(BF16), lim, e1, r2, e2)
        x2 = _peer_residual(x2, peer_t, mod3, 5, norm_f_g, seq, final_norm=(l == depth - 1))
    return x2.reshape(nb, seq, d)
```

```python
import functools

import numpy as np
import jax
import jax.numpy as jnp
from jax import lax
from jax.experimental import pallas as pl
from jax.experimental.pallas import tpu as pltpu

GRID_W = 64
RMS_EPS = 1e-6
PEER_TOPK = 16
LANES = 128
BF16_ROWS = 16
V7X_VMEM_BYTES = 64 * 1024 * 1024
VMEM_LIMIT = V7X_VMEM_BYTES - 8 * 1024 * 1024
ATTN_ROW_BLOCK = 8
ATTN_GROUP_ROWS = 4
MASKED = -1e30

F32 = jnp.float32
BF16 = jnp.bfloat16


def _cparams(*sem):
    return pltpu.CompilerParams(dimension_semantics=sem, vmem_limit_bytes=VMEM_LIMIT)


def _tile(n, target, align=LANES):
    best = None
    for cand in range(align, min(n, target) + 1, align):
        if n % cand == 0:
            best = cand
    assert best is not None, (n, target, align)
    return best


def _mod_kernel(cb_ref, w_ref, b_ref, o_ref):
    nb = cb_ref.shape[0]
    tn = w_ref.shape[1]
    for j in range(tn // LANES):
        cols = slice(j * LANES, (j + 1) * LANES)
        w = w_ref[:, cols]
        for b in range(nb):
            o_ref[b:b + 1, cols] = jnp.sum(w * cb_ref[b], axis=0, keepdims=True) + b_ref[:, cols]


def _mod(c, w_ada, b_ada, tn=512):
    nb, d = c.shape
    n = w_ada.shape[1]
    cb = jnp.broadcast_to(c[:, :, None], (nb, d, LANES))
    return pl.pallas_call(
        _mod_kernel,
        grid=(n // tn,),
        in_specs=[pl.BlockSpec((nb, d, LANES), lambda j: (0, 0, 0)),
                  pl.BlockSpec((d, tn), lambda j: (0, j)),
                  pl.BlockSpec((1, tn), lambda j: (0, j))],
        out_specs=pl.BlockSpec((nb, tn), lambda j: (0, j)),
        out_shape=jax.ShapeDtypeStruct((nb, n), F32),
        compiler_params=_cparams("arbitrary"),
        name="ada_mod",
    )(cb, w_ada, b_ada.reshape(1, n))


def _norm_mod_kernel(x_ref, g_ref, mod_ref, o_ref, *ot_ref, shift_row):
    x = x_ref[...]
    y = x * lax.rsqrt(jnp.mean(x * x, axis=-1, keepdims=True) + RMS_EPS) * g_ref[...]
    shift = mod_ref[shift_row:shift_row + 1, :]
    scale = mod_ref[shift_row + 1:shift_row + 2, :]
    h = y * (1.0 + scale) + shift
    o_ref[...] = h.astype(o_ref.dtype)
    if ot_ref:
        ot_ref[0][...] = h.T.astype(o_ref.dtype)


def _norm_mod(x2, g, mod3, shift_row, seq, with_transposed=False, tm=256):
    t, d = x2.shape
    per_seq = seq // tm
    out_specs = [pl.BlockSpec((tm, d), lambda i: (i, 0))]
    out_shape = [jax.ShapeDtypeStruct((t, d), BF16)]
    if with_transposed:
        out_specs.append(pl.BlockSpec((d, tm), lambda i: (0, i)))
        out_shape.append(jax.ShapeDtypeStruct((d, t), BF16))
    out = pl.pallas_call(
        functools.partial(_norm_mod_kernel, shift_row=shift_row),
        grid=(t // tm,),
        in_specs=[pl.BlockSpec((tm, d), lambda i: (i, 0)),
                  pl.BlockSpec((1, d), lambda i: (0, 0)),
                  pl.BlockSpec((None, mod3.shape[1], d), lambda i: (i // per_seq, 0, 0))],
        out_specs=out_specs,
        out_shape=out_shape,
        compiler_params=_cparams("parallel"),
        name="norm_mod",
    )(x2, g.reshape(1, d), mod3)
    return out if with_transposed else out[0]


def _mm_kernel(a_ref, w_ref, o_ref):
    o_ref[...] = jnp.dot(a_ref[...], w_ref[...], preferred_element_type=F32).astype(o_ref.dtype)


def _mm_chunked_kernel(a_ref, w_ref, o_ref):
    r = jnp.dot(a_ref[...], w_ref[...], preferred_element_type=F32)
    for c in range(o_ref.shape[0]):
        o_ref[c] = r[:, c * LANES:(c + 1) * LANES].astype(o_ref.dtype)


def _mm_cast_kernel(a_ref, w_ref, u_ref, v_ref, o_ref, ub_ref, vt_ref):
    o_ref[...] = jnp.dot(a_ref[...], w_ref[...], preferred_element_type=F32).astype(o_ref.dtype)
    ub_ref[...] = u_ref[...].astype(ub_ref.dtype)
    vt_ref[...] = v_ref[...].T.astype(vt_ref.dtype)


def _matmul(a, w, tm, tn, name, col0=0, n=None, chunked=False, cast_tables=None):
    m, k = a.shape
    n = w.shape[1] - col0 if n is None else n
    tm, tn = _tile(m, tm), _tile(int(np.gcd(n, col0)), tn)
    j0 = col0 // tn
    grid = (m // tm, n // tn)
    in_specs = [pl.BlockSpec((tm, k), lambda i, j: (i, 0)),
                pl.BlockSpec((k, tn), lambda i, j: (0, j0 + j))]
    if chunked:
        return pl.pallas_call(
            _mm_chunked_kernel, grid=grid, in_specs=in_specs,
            out_specs=pl.BlockSpec((tn // LANES, tm, LANES), lambda i, j: (j, i, 0)),
            out_shape=jax.ShapeDtypeStruct((n // LANES, m, LANES), BF16),
            compiler_params=_cparams("parallel", "arbitrary"), name=name,
        )(a, w)
    out_spec = pl.BlockSpec((tm, tn), lambda i, j: (i, j))
    out_shape = jax.ShapeDtypeStruct((m, n), BF16)
    if cast_tables is None:
        return pl.pallas_call(
            _mm_kernel, grid=grid, in_specs=in_specs, out_specs=out_spec, out_shape=out_shape,
            compiler_params=_cparams("parallel", "arbitrary"), name=name,
        )(a, w)
    u, v = cast_tables
    ne, d = u.shape
    steps = grid[0] * grid[1]
    rows = min(r for r in range(LANES, ne + 1, LANES) if ne % r == 0 and ne // r <= steps)
    last = ne // rows - 1

    def block(i, j):
        return jnp.minimum(i * grid[1] + j, last)

    return pl.pallas_call(
        _mm_cast_kernel, grid=grid,
        in_specs=in_specs + [pl.BlockSpec((rows, d), lambda i, j: (block(i, j), 0)),
                             pl.BlockSpec((rows, d), lambda i, j: (block(i, j), 0))],
        out_specs=[out_spec,
                   pl.BlockSpec((rows, d), lambda i, j: (block(i, j), 0)),
                   pl.BlockSpec((d, rows), lambda i, j: (0, block(i, j)))],
        out_shape=[out_shape, jax.ShapeDtypeStruct((ne, d), BF16), jax.ShapeDtypeStruct((d, ne), BF16)],
        compiler_params=_cparams("arbitrary", "arbitrary"), name=name,
    )(a, w, u, v)


def _attn_geometry(rows, win_rows):
    ga = ATTN_GROUP_ROWS
    kh = min(win_rows, rows)
    gkr = min(ga + kh, rows)
    assert rows % ATTN_ROW_BLOCK == 0 and ATTN_ROW_BLOCK % ga == 0 and ga >= kh // 2, (rows, ga, kh)
    return ga, kh, gkr


def _attn_col_bias(rpb):
    nh, nr, nc = rpb.shape
    win_cols = (nc + 1) // 2
    cols = np.arange(GRID_W)
    cs = np.clip(cols - win_cols // 2, 0, GRID_W - win_cols)
    kc = cols[None, :]
    cvalid = (kc >= cs[:, None]) & (kc < cs[:, None] + win_cols)
    dc = np.clip(kc - cols[:, None] + (win_cols - 1), 0, nc - 1)
    onehot = (dc[None] == np.arange(nc)[:, None, None]).astype(np.float32)
    cb = jnp.einsum('hrm,mqc->hrqc', rpb, jnp.asarray(onehot), precision=lax.Precision.HIGHEST)
    return jnp.where(jnp.asarray(cvalid), cb, MASKED)


def _attn_kernel(q_ref, k_ref, v_ref, cb_ref, o_ref, bias_ref, *, rows, win_rows, scale):
    ga, kh, gkr = _attn_geometry(rows, win_rows)
    gq, gk = ga * GRID_W, gkr * GRID_W
    first_rows = (0, min(ga, rows - ga), rows - ga)

    @pl.when((pl.program_id(1) == 0) & (pl.program_id(2) == 0))
    def _():
        for kind, first in enumerate(first_rows):
            ws = int(np.clip(first - kh // 2, 0, rows - gkr))
            for a in range(ga):
                r = first + a
                rs = int(np.clip(r - kh // 2, 0, rows - kh))
                for i in range(gkr):
                    krow = ws + i
                    dst = (kind, slice(a * GRID_W, (a + 1) * GRID_W), slice(i * GRID_W, (i + 1) * GRID_W))
                    if rs <= krow < rs + kh:
                        bias_ref[dst] = cb_ref[krow - r + (win_rows - 1)]
                    else:
                        bias_ref[dst] = jnp.full((GRID_W, GRID_W), MASKED, F32)

    r0 = pl.program_id(2) * ATTN_ROW_BLOCK
    groups = []
    for g in range(ATTN_ROW_BLOCK // ga):
        first = r0 + g * ga
        start = pl.multiple_of(jnp.clip(first - kh // 2, 0, rows - gkr) * GRID_W, GRID_W)
        kind = jnp.where(first == 0, 0, jnp.where(first == rows - ga, 2, 1))
        k = k_ref[pl.ds(start, gk), :]
        s = lax.dot_general(q_ref[g * gq:(g + 1) * gq, :], k, (((1,), (1,)), ((), ())),
                            preferred_element_type=F32)
        groups.append((start, kind, s))
    for g, (start, kind, s) in enumerate(groups):
        s = s * scale + bias_ref[kind]
        p = jnp.exp(s - jnp.max(s, axis=-1, keepdims=True))
        denom = jnp.sum(p, axis=-1, keepdims=True)
        o = jnp.dot(p.astype(v_ref.dtype), v_ref[pl.ds(start, gk), :], preferred_element_type=F32)
        o_ref[g * gq:(g + 1) * gq, :] = (o / denom).astype(o_ref.dtype)


def _attention(qkv, col_bias, nb, seq):
    n_heads, head_dim = qkv.shape[0] // 3, qkv.shape[2]
    rows = seq // GRID_W
    nr = col_bias.shape[1]
    win_rows = (nr + 1) // 2
    ga, kh, gkr = _attn_geometry(rows, win_rows)
    nrb = rows // ATTN_ROW_BLOCK
    tq = ATTN_ROW_BLOCK * GRID_W
    return pl.pallas_call(
        functools.partial(_attn_kernel, rows=rows, win_rows=win_rows, scale=head_dim ** -0.5),
        grid=(n_heads, nb, nrb),
        in_specs=[pl.BlockSpec((None, tq, head_dim), lambda h, b, r: (h, b * nrb + r, 0)),
                  pl.BlockSpec((None, seq, head_dim), lambda h, b, r: (n_heads + h, b, 0)),
                  pl.BlockSpec((None, seq, head_dim), lambda h, b, r: (2 * n_heads + h, b, 0)),
                  pl.BlockSpec((None, nr, GRID_W, GRID_W), lambda h, b, r: (h, 0, 0, 0))],
        out_specs=pl.BlockSpec((tq, head_dim), lambda h, b, r: (b * nrb + r, h)),
        out_shape=jax.ShapeDtypeStruct((nb * seq, n_heads * head_dim), BF16),
        scratch_shapes=[pltpu.VMEM((3, ga * GRID_W, gkr * GRID_W), F32)],
        compiler_params=_cparams("arbitrary", "arbitrary", "arbitrary"),
        name="na2d_attention",
    )(qkv, qkv, qkv, col_bias)


def _conv_kernel(cb_ref, cc_ref, ch_ref, ccp_ref, chp_ref, ccn_ref, chn_ref, w_ref, o_ref, *, per_seq):
    i = pl.program_id(0)
    ts = cc_ref.shape[0]
    u = cc_ref[...].astype(F32) * ch_ref[...].astype(F32)
    last_halo = BF16_ROWS - 1
    u_before = ccp_ref[last_halo:, :].astype(F32) * chp_ref[last_halo:, :].astype(F32)
    u_after = ccn_ref[0:1, :].astype(F32) * chn_ref[0:1, :].astype(F32)
    u_before = jnp.where(i % per_seq == 0, 0.0, u_before)
    u_after = jnp.where(i % per_seq == per_seq - 1, 0.0, u_after)
    row = lax.broadcasted_iota(jnp.int32, u.shape, 0)
    u_prev = jnp.where(row == 0, u_before, pltpu.roll(u, 1, axis=0))
    u_next = jnp.where(row == ts - 1, u_after, pltpu.roll(u, ts - 1, axis=0))
    y = w_ref[0:1, :] * u_prev + w_ref[1:2, :] * u + w_ref[2:3, :] * u_next
    o_ref[...] = (cb_ref[...].astype(F32) * y).astype(o_ref.dtype)


def _short_conv(proj, conv_w, seq, col0, cw, ts=512, tc=512):
    t = proj.shape[0]
    ts, tc = _tile(seq, ts), _tile(np.gcd(cw, col0), tc)
    per_seq = seq // ts
    c0 = col0 // tc
    nct = cw // tc
    hb = ts // BF16_ROWS
    last_hb = t // BF16_ROWS - 1

    def main(k):
        return pl.BlockSpec((ts, tc), lambda i, j: (i, c0 + k * nct + j))

    def before(k):
        return pl.BlockSpec((BF16_ROWS, tc), lambda i, j: (jnp.maximum(i * hb - 1, 0), c0 + k * nct + j))

    def after(k):
        return pl.BlockSpec((BF16_ROWS, tc), lambda i, j: (jnp.minimum((i + 1) * hb, last_hb), c0 + k * nct + j))

    return pl.pallas_call(
        functools.partial(_conv_kernel, per_seq=per_seq),
        grid=(t // ts, nct),
        in_specs=[main(0), main(1), main(2), before(1), before(2), after(1), after(2),
                  pl.BlockSpec((conv_w.shape[0], tc), lambda i, j: (0, j))],
        out_specs=pl.BlockSpec((ts, tc), lambda i, j: (i, j)),
        out_shape=jax.ShapeDtypeStruct((t, cw), BF16),
        compiler_params=_cparams("parallel", "parallel"),
        name="short_conv",
    )(proj, proj, proj, proj, proj, proj, proj, conv_w)


def _merge_kernel(a_ref, c_ref, wa_ref, wb_ref, ga_ref, gb_ref, o_ref):
    ya = jnp.dot(a_ref[...], wa_ref[...], preferred_element_type=F32)
    yb = jnp.dot(c_ref[...], wb_ref[...], preferred_element_type=F32)
    ga = jax.nn.sigmoid(ga_ref[...].astype(F32))
    gb = jax.nn.sigmoid(gb_ref[...].astype(F32))
    o_ref[...] = (ga * ya + gb * yb).astype(o_ref.dtype)


def _merge(attn, convm, wa, wb, proj, gate_col0, tm=1024, tn=1024):
    t, d = attn.shape[0], wa.shape[1]
    tm, tn = _tile(t, tm), _tile(np.gcd(d, gate_col0), tn)
    g0 = gate_col0 // tn
    nt = d // tn
    return pl.pallas_call(
        _merge_kernel,
        grid=(t // tm, nt),
        in_specs=[pl.BlockSpec((tm, attn.shape[1]), lambda i, j: (i, 0)),
                  pl.BlockSpec((tm, convm.shape[1]), lambda i, j: (i, 0)),
                  pl.BlockSpec((wa.shape[0], tn), lambda i, j: (0, j)),
                  pl.BlockSpec((wb.shape[0], tn), lambda i, j: (0, j)),
                  pl.BlockSpec((tm, tn), lambda i, j: (i, g0 + j)),
                  pl.BlockSpec((tm, tn), lambda i, j: (i, g0 + nt + j))],
        out_specs=pl.BlockSpec((tm, tn), lambda i, j: (i, j)),
        out_shape=jax.ShapeDtypeStruct((t, d), BF16),
        compiler_params=_cparams("parallel", "arbitrary"),
        name="merge_branches",
    )(attn, convm, wa, wb, proj, proj)


def _wo_kernel(m_ref, w_ref, x_ref, mod_ref, o_ref, *, gate_row):
    y = jnp.dot(m_ref[...], w_ref[...], preferred_element_type=F32)
    o_ref[...] = x_ref[...] + mod_ref[gate_row:gate_row + 1, :] * y


def _wo_residual(merged, wo, x2, mod3, gate_row, seq, tm=1024, tn=1024):
    t, d = x2.shape
    tm, tn = _tile(seq, tm), _tile(d, tn)
    per_seq = seq // tm
    return pl.pallas_call(
        functools.partial(_wo_kernel, gate_row=gate_row),
        grid=(t // tm, d // tn),
        in_specs=[pl.BlockSpec((tm, merged.shape[1]), lambda i, j: (i, 0)),
                  pl.BlockSpec((wo.shape[0], tn), lambda i, j: (0, j)),
                  pl.BlockSpec((tm, tn), lambda i, j: (i, j)),
                  pl.BlockSpec((None, mod3.shape[1], tn), lambda i, j: (i // per_seq, 0, j))],
        out_specs=pl.BlockSpec((tm, tn), lambda i, j: (i, j)),
        out_shape=jax.ShapeDtypeStruct((t, d), F32),
        compiler_params=_cparams("parallel", "arbitrary"),
        name="wo_residual",
    )(merged, wo, x2, mod3)


def _candidate_pairs():
    k = PEER_TOPK
    return [(p, q) for p in range(k) for q in range(k) if (p + 1) * (q + 1) <= k]


_RANK_CODE = 2.0 ** 100


def _top16_exact(s, iota):
    n = s.shape[0]
    rank = jnp.full(s.shape, float(PEER_TOPK), F32)
    vals = []
    for it in range(PEER_TOPK):
        m = jnp.max(s, axis=0, keepdims=True)
        idx = jnp.min(jnp.where(s == m, iota, float(n)), axis=0, keepdims=True)
        sel = iota == idx
        rank = jnp.where(sel, float(it), rank)
        s = jnp.where(sel, -jnp.inf, s)
        vals.append(m)
    return rank, vals


def _top16_fast(s):
    vals = []
    for it in range(PEER_TOPK):
        m = jnp.max(s, axis=0, keepdims=True)
        s = jnp.where(s == m, -_RANK_CODE * (1.0 + it / PEER_TOPK), s)
        vals.append(m)
    is_taken = s <= -_RANK_CODE
    rank = jnp.where(is_taken, (s * (-1.0 / _RANK_CODE) - 1.0) * PEER_TOPK, float(PEER_TOPK))
    taken = jnp.sum(jnp.where(is_taken, 1.0, 0.0), axis=0, keepdims=True)
    return rank, vals, taken


def _peer_select_kernel(q_ref, k1_ref, k2_ref, lim_ref, e1_ref, r2_ref, e2_ref, s1_ref, s2_ref, a_ref, b_ref):
    nh, n, dk = k1_ref.shape
    tm = q_ref.shape[0]
    nt = (((1,), (1,)), ((), ()))
    most_taken = jnp.zeros((1, tm), F32)
    for h in range(nh):
        q1 = q_ref[:, (2 * h) * dk:(2 * h + 1) * dk]
        q2 = q_ref[:, (2 * h + 1) * dk:(2 * h + 2) * dk]
        s1 = lax.dot_general(k1_ref[h], q1, nt, preferred_element_type=F32)
        s2 = lax.dot_general(k2_ref[h], q2, nt, preferred_element_type=F32)
        s1_ref[h] = s1
        s2_ref[h] = s2
        rank1, a, taken1 = _top16_fast(s1)
        rank2, b, taken2 = _top16_fast(s2)
        lim_ref[h] = rank1
        r2_ref[h] = rank2.astype(r2_ref.dtype)
        for p in range(PEER_TOPK):
            a_ref[p * nh + h:p * nh + h + 1, :] = a[p]
            b_ref[p * nh + h:p * nh + h + 1, :] = b[p]
        most_taken = jnp.maximum(most_taken, jnp.maximum(taken1, taken2))

    @pl.when(jnp.max(most_taken) > float(PEER_TOPK))
    def _():
        iota = lax.broadcasted_iota(jnp.int32, (n, tm), 0).astype(F32)

        def redo(h, carry):
            rank1, a = _top16_exact(s1_ref[h], iota)
            rank2, b = _top16_exact(s2_ref[h], iota)
            lim_ref[h] = rank1
            r2_ref[h] = rank2.astype(r2_ref.dtype)
            for p in range(PEER_TOPK):
                a_ref[pl.ds(p * nh + h, 1), :] = a[p]
                b_ref[pl.ds(p * nh + h, 1), :] = b[p]
            return carry

        lax.fori_loop(0, nh, redo, 0)

    big_a = [a_ref[p * nh:(p + 1) * nh, :] for p in range(PEER_TOPK)]
    big_b = [b_ref[q * nh:(q + 1) * nh, :] for q in range(PEER_TOPK)]
    pairs = _candidate_pairs()
    val = [big_a[p] + big_b[q] for p, q in pairs]
    beaten = []
    for c, (p, q) in enumerate(pairs):
        beaten.append(jnp.full((nh, tm), float((p + 1) * (q + 1) - 1), F32))
    for c1, (p1, q1) in enumerate(pairs):
        for c2 in range(c1 + 1, len(pairs)):
            p2, q2 = pairs[c2]
            if (p1 <= p2 and q1 <= q2) or (p2 <= p1 and q2 <= q1):
                continue
            first_wins = jnp.where(val[c1] >= val[c2], 1.0, 0.0)
            beaten[c2] = beaten[c2] + first_wins
            beaten[c1] = beaten[c1] + (1.0 - first_wins)
    count = [jnp.zeros((nh, tm), F32) for _ in range(PEER_TOPK)]
    z = jnp.zeros((nh, tm), F32)
    for c, (p, q) in enumerate(pairs):
        chosen = beaten[c] < float(PEER_TOPK)
        count[p] = count[p] + jnp.where(chosen, 1.0, 0.0)
        z = z + jnp.where(chosen, jnp.exp((big_a[p] - big_a[0]) + (big_b[q] - big_b[0])), 0.0)
    inv_z = 1.0 / z
    for h in range(nh):
        rank1 = lim_ref[h]
        lim = jnp.zeros((n, tm), F32)
        for p in range(PEER_TOPK):
            lim = jnp.where(rank1 == float(p), count[p][h:h + 1, :], lim)
        lim_ref[h] = lim
        e1_ref[h] = jnp.exp(s1_ref[h] - big_a[0][h:h + 1, :])
        e2_ref[h] = (jnp.exp(s2_ref[h] - big_b[0][h:h + 1, :]) * inv_z[h:h + 1, :]).astype(e2_ref.dtype)


def _peer_select(qp, k1, k2, tm=256):
    t = qp.shape[0]
    nh, n, dk = k1.shape
    out = jax.ShapeDtypeStruct((nh, n, t), F32)
    out16 = jax.ShapeDtypeStruct((nh, n, t), BF16)
    ospec = pl.BlockSpec((nh, n, tm), lambda i: (0, 0, i))
    return pl.pallas_call(
        _peer_select_kernel,
        grid=(t // tm,),
        in_specs=[pl.BlockSpec((tm, qp.shape[1]), lambda i: (i, 0)),
                  pl.BlockSpec((nh, n, dk), lambda i: (0, 0, 0)),
                  pl.BlockSpec((nh, n, dk), lambda i: (0, 0, 0))],
        out_specs=[ospec, ospec, ospec, ospec],
        out_shape=[out, out, out16, out16],
        scratch_shapes=[pltpu.VMEM((nh, n, tm), F32), pltpu.VMEM((nh, n, tm), F32),
                        pltpu.VMEM((PEER_TOPK * nh, tm), F32), pltpu.VMEM((PEER_TOPK * nh, tm), F32)],
        compiler_params=_cparams("parallel"),
        name="peer_select",
    )(qp, k1, k2)


F32_ROWS = 8
PEER_SUBTILE = 512


def _peer_dense_kernel(h_ref, u_ref, vt_ref, lim_ref, e1_ref, r2_ref, e2_ref, o_ref):
    te = u_ref.shape[0]
    nh, n, _ = r2_ref.shape
    per_sub = PEER_SUBTILE // n

    @pl.when(pl.program_id(1) == 0)
    def _():
        o_ref[...] = jnp.zeros_like(o_ref)

    def gate(c):
        g = None
        for h in range(nh):
            lim = lim_ref[h, c:c + 1, :].astype(BF16)
            e1 = e1_ref[h, c:c + 1, :].astype(BF16)
            term = jnp.where(r2_ref[h] < lim, e2_ref[h] * e1, jnp.zeros((), BF16))
            g = term if g is None else g + term
        return g

    hh = h_ref[...]
    subs = [slice(s * PEER_SUBTILE, (s + 1) * PEER_SUBTILE) for s in range(te // PEER_SUBTILE)]
    acts = [jnp.dot(u_ref[rows, :], hh, preferred_element_type=F32) for rows in subs]
    for s, rows in enumerate(subs):
        g = jnp.concatenate([gate(s * per_sub + c) for c in range(per_sub)], axis=0)
        ga = g * jax.nn.gelu(acts[s]).astype(BF16)
        o_ref[...] += jnp.dot(vt_ref[:, rows], ga, preferred_element_type=F32)


def _peer_dense(h2t, u, vt, lim, e1, r2, e2, tm=512):
    d, t = h2t.shape
    ne = u.shape[0]
    nh, n, _ = lim.shape
    tm = _tile(t, tm)
    te = F32_ROWS * n
    assert te % PEER_SUBTILE == 0 and PEER_SUBTILE % n == 0
    once = pl.Buffered(1)
    rows = pl.BlockSpec((nh, F32_ROWS, tm), lambda i, e: (0, e, i))
    full = pl.BlockSpec((nh, n, tm), lambda i, e: (0, 0, i))
    return pl.pallas_call(
        _peer_dense_kernel,
        grid=(t // tm, ne // te),
        in_specs=[pl.BlockSpec((d, tm), lambda i, e: (0, i), pipeline_mode=once),
                  pl.BlockSpec((te, d), lambda i, e: (e, 0)),
                  pl.BlockSpec((d, te), lambda i, e: (0, e)),
                  rows, rows, full, full],
        out_specs=pl.BlockSpec((d, tm), lambda i, e: (0, i), pipeline_mode=once),
        out_shape=jax.ShapeDtypeStruct((d, t), F32),
        compiler_params=_cparams("parallel", "arbitrary"),
        name="peer_dense",
    )(h2t, u, vt, lim, e1, r2, e2)


def _peer_residual_kernel(x_ref, pt_ref, mod_ref, g_ref, o_ref, *, gate_row, final_norm):
    y = x_ref[...] + mod_ref[gate_row:gate_row + 1, :] * pt_ref[...].T
    if final_norm:
        y = y * lax.rsqrt(jnp.mean(y * y, axis=-1, keepdims=True) + RMS_EPS) * g_ref[...]
    o_ref[...] = y


def _peer_residual(x1, peer_t, mod3, gate_row, g, seq, final_norm, tm=256):
    t, d = x1.shape
    per_seq = seq // tm
    return pl.pallas_call(
        functools.partial(_peer_residual_kernel, gate_row=gate_row, final_norm=final_norm),
        grid=(t // tm,),
        in_specs=[pl.BlockSpec((tm, d), lambda i: (i, 0)),
                  pl.BlockSpec((d, tm), lambda i: (0, i)),
                  pl.BlockSpec((None, mod3.shape[1], d), lambda i: (i // per_seq, 0, 0)),
                  pl.BlockSpec((1, d), lambda i: (0, 0))],
        out_specs=pl.BlockSpec((tm, d), lambda i: (i, 0)),
        out_shape=jax.ShapeDtypeStruct((t, d), F32),
        compiler_params=_cparams("parallel"),
        name="peer_residual",
    )(x1, peer_t, mod3, g.reshape(1, d))


def kernel(x, c, w_ada, b_ada, norm1_g, w_in, rpb, conv_w, w_attn_out, w_conv_out, w_o, norm2_g,
           w_q_peer, sub_keys_1, sub_keys_2, expert_u, expert_v, norm_f_g):
    nb, seq, d = x.shape
    depth = w_ada.shape[0]
    n_mod = w_ada.shape[2] // d
    aw = w_attn_out.shape[1]
    cw = w_conv_out.shape[1]
    x2 = x.reshape(nb * seq, d)
    for l in range(depth):
        mod3 = _mod(c, w_ada[l], b_ada[l]).reshape(nb, n_mod, d)

        h1 = _norm_mod(x2, norm1_g[l], mod3, 0, seq)
        w_in16 = w_in[l].astype(BF16)
        qkv = _matmul(h1, w_in16, 1024, 1024, "in_proj_qkv", n=3 * aw, chunked=True)
        rest, u16, vt16 = _matmul(h1, w_in16, 1024, 1024, "in_proj_rest", col0=3 * aw,
                                  cast_tables=(expert_u[l], expert_v[l]))
        attn = _attention(qkv, _attn_col_bias(rpb[l]), nb, seq)
        convm = _short_conv(rest, conv_w[l], seq, 0, cw)
        merged = _merge(attn, convm, w_attn_out[l].astype(BF16), w_conv_out[l].astype(BF16), rest, 3 * cw)
        x2 = _wo_residual(merged, w_o[l].astype(BF16), x2, mod3, 2, seq)

        h2, h2t = _norm_mod(x2, norm2_g[l], mod3, 3, seq, with_transposed=True)
        qp = _matmul(h2, w_q_peer[l].astype(BF16), 1024, 1024, "peer_query")
        lim, e1, r2, e2 = _peer_select(qp, sub_keys_1[l].astype(BF16), sub_keys_2[l].astype(BF16))
        peer_t = _peer_dense(h2t, u16, vt16, lim, e1, r2, e2)
        x2 = _peer_residual(x2, peer_t, mod3, 5, norm_f_g, seq, final_norm=(l == depth - 1))
    return x2.reshape(nb, seq, d)
```

```python
import functools

import numpy as np
import jax
import jax.numpy as jnp
from jax import lax
from jax.experimental import pallas as pl
from jax.experimental.pallas import tpu as pltpu

GRID_W = 64
RMS_EPS = 1e-6
PEER_TOPK = 16
LANES = 128
BF16_ROWS = 16
V7X_VMEM_BYTES = 64 * 1024 * 1024
VMEM_LIMIT = V7X_VMEM_BYTES - 8 * 1024 * 1024
ATTN_ROW_BLOCK = 32
ATTN_GROUP_ROWS = 4
MASKED = -1e30

F32 = jnp.float32
BF16 = jnp.bfloat16


def _cparams(*sem):
    return pltpu.CompilerParams(dimension_semantics=sem, vmem_limit_bytes=VMEM_LIMIT)


def _tile(n, target, align=LANES):
    best = None
    for cand in range(align, min(n, target) + 1, align):
        if n % cand == 0:
            best = cand
    assert best is not None, (n, target, align)
    return best


def _mod_kernel(cb_ref, w_ref, b_ref, o_ref):
    nb = cb_ref.shape[0]
    tn = w_ref.shape[1]
    for j in range(tn // LANES):
        cols = slice(j * LANES, (j + 1) * LANES)
        w = w_ref[:, cols]
        for b in range(nb):
            o_ref[b:b + 1, cols] = jnp.sum(w * cb_ref[b], axis=0, keepdims=True) + b_ref[:, cols]


def _mod(c, w_ada, b_ada, tn=512):
    nb, d = c.shape
    n = w_ada.shape[1]
    cb = jnp.broadcast_to(c[:, :, None], (nb, d, LANES))
    return pl.pallas_call(
        _mod_kernel,
        grid=(n // tn,),
        in_specs=[pl.BlockSpec((nb, d, LANES), lambda j: (0, 0, 0)),
                  pl.BlockSpec((d, tn), lambda j: (0, j)),
                  pl.BlockSpec((1, tn), lambda j: (0, j))],
        out_specs=pl.BlockSpec((nb, tn), lambda j: (0, j)),
        out_shape=jax.ShapeDtypeStruct((nb, n), F32),
        compiler_params=_cparams("arbitrary"),
        name="ada_mod",
    )(cb, w_ada, b_ada.reshape(1, n))


def _norm_mod_kernel(x_ref, g_ref, mod_ref, o_ref, *ot_ref, shift_row):
    x = x_ref[...]
    y = x * lax.rsqrt(jnp.mean(x * x, axis=-1, keepdims=True) + RMS_EPS) * g_ref[...]
    shift = mod_ref[shift_row:shift_row + 1, :]
    scale = mod_ref[shift_row + 1:shift_row + 2, :]
    h = y * (1.0 + scale) + shift
    o_ref[...] = h.astype(o_ref.dtype)
    if ot_ref:
        ot_ref[0][...] = h.T.astype(o_ref.dtype)


def _norm_mod(x2, g, mod3, shift_row, seq, with_transposed=False, tm=512):
    t, d = x2.shape
    per_seq = seq // tm
    out_specs = [pl.BlockSpec((tm, d), lambda i: (i, 0))]
    out_shape = [jax.ShapeDtypeStruct((t, d), BF16)]
    if with_transposed:
        out_specs.append(pl.BlockSpec((d, tm), lambda i: (0, i)))
        out_shape.append(jax.ShapeDtypeStruct((d, t), BF16))
    out = pl.pallas_call(
        functools.partial(_norm_mod_kernel, shift_row=shift_row),
        grid=(t // tm,),
        in_specs=[pl.BlockSpec((tm, d), lambda i: (i, 0)),
                  pl.BlockSpec((1, d), lambda i: (0, 0)),
                  pl.BlockSpec((None, mod3.shape[1], d), lambda i: (i // per_seq, 0, 0))],
        out_specs=out_specs,
        out_shape=out_shape,
        compiler_params=_cparams("parallel"),
        name="norm_mod",
    )(x2, g.reshape(1, d), mod3)
    return out if with_transposed else out[0]


def _mm_kernel(a_ref, w_ref, *refs, chunked, transposed):
    n_side = len(transposed)
    o_ref = refs[n_side]
    r = jnp.dot(a_ref[...], w_ref[...], preferred_element_type=F32)
    if chunked:
        for c in range(o_ref.shape[0]):
            o_ref[c] = r[:, c * LANES:(c + 1) * LANES].astype(o_ref.dtype)
    else:
        o_ref[...] = r.astype(o_ref.dtype)
    for src, dst, tr in zip(refs[:n_side], refs[n_side + 1:], transposed):
        dst[...] = (src[...].T if tr else src[...]).astype(dst.dtype)


def _matmul(a, w, tm, tn, name, col0=0, n=None, chunked=False, cast_jobs=()):
    m, k = a.shape
    n = w.shape[1] - col0 if n is None else n
    tm, tn = _tile(m, tm), _tile(int(np.gcd(n, col0)), tn)
    j0 = col0 // tn
    grid = (m // tm, n // tn)
    steps = grid[0] * grid[1]
    in_specs = [pl.BlockSpec((tm, k), lambda i, j: (i, 0)),
                pl.BlockSpec((k, tn), lambda i, j: (0, j0 + j))]
    if chunked:
        out_specs = [pl.BlockSpec((tn // LANES, tm, LANES), lambda i, j: (j, i, 0))]
        out_shape = [jax.ShapeDtypeStruct((n // LANES, m, LANES), BF16)]
    else:
        out_specs = [pl.BlockSpec((tm, tn), lambda i, j: (i, j))]
        out_shape = [jax.ShapeDtypeStruct((m, n), BF16)]
    for table, tr in cast_jobs:
        nr, nc = table.shape
        align = LANES if tr else BF16_ROWS
        rows = min(r for r in range(align, nr + 1, align) if nr % r == 0 and nr // r <= steps)

        def block(i, j, last=nr // rows - 1):
            return jnp.minimum(i * grid[1] + j, last)

        in_specs.append(pl.BlockSpec((rows, nc), lambda i, j, block=block: (block(i, j), 0)))
        if tr:
            out_specs.append(pl.BlockSpec((nc, rows), lambda i, j, block=block: (0, block(i, j))))
            out_shape.append(jax.ShapeDtypeStruct((nc, nr), BF16))
        else:
            out_specs.append(pl.BlockSpec((rows, nc), lambda i, j, block=block: (block(i, j), 0)))
            out_shape.append(jax.ShapeDtypeStruct((nr, nc), BF16))
    out = pl.pallas_call(
        functools.partial(_mm_kernel, chunked=chunked, transposed=tuple(tr for _, tr in cast_jobs)),
        grid=grid, in_specs=in_specs, out_specs=out_specs, out_shape=out_shape,
        compiler_params=_cparams("arbitrary", "arbitrary"), name=name,
    )(a, w, *[table for table, _ in cast_jobs])
    return out if cast_jobs else out[0]


def _attn_geometry(rows, win_rows):
    ga = ATTN_GROUP_ROWS
    kh = min(win_rows, rows)
    gkr = min(ga + kh, rows)
    rb = min(ATTN_ROW_BLOCK, rows)
    assert rows % rb == 0 and rb % ga == 0 and ga >= kh // 2, (rows, rb, ga, kh)
    return rb, ga, kh, gkr


def _attn_col_bias(rpb):
    nh, nr, nc = rpb.shape
    win_cols = (nc + 1) // 2
    cols = np.arange(GRID_W)
    cs = np.clip(cols - win_cols // 2, 0, GRID_W - win_cols)
    kc = cols[None, :]
    cvalid = (kc >= cs[:, None]) & (kc < cs[:, None] + win_cols)
    dc = np.clip(kc - cols[:, None] + (win_cols - 1), 0, nc - 1)
    onehot = (dc[None] == np.arange(nc)[:, None, None]).astype(np.float32)
    cb = jnp.einsum('hrm,mqc->hrqc', rpb, jnp.asarray(onehot), precision=lax.Precision.HIGHEST)
    return jnp.where(jnp.asarray(cvalid), cb, MASKED)


def _attn_kernel(q_ref, k_ref, v_ref, cb_ref, o_ref, bias_ref, *, rows, win_rows, scale):
    rb, ga, kh, gkr = _attn_geometry(rows, win_rows)
    gq, gk = ga * GRID_W, gkr * GRID_W
    first_rows = (0, min(ga, rows - ga), rows - ga)

    @pl.when((pl.program_id(1) == 0) & (pl.program_id(2) == 0))
    def _():
        for kind, first in enumerate(first_rows):
            ws = int(np.clip(first - kh // 2, 0, rows - gkr))
            for a in range(ga):
                r = first + a
                rs = int(np.clip(r - kh // 2, 0, rows - kh))
                for i in range(gkr):
                    krow = ws + i
                    dst = (kind, slice(a * GRID_W, (a + 1) * GRID_W), slice(i * GRID_W, (i + 1) * GRID_W))
                    if rs <= krow < rs + kh:
                        bias_ref[dst] = cb_ref[krow - r + (win_rows - 1)]
                    else:
                        bias_ref[dst] = jnp.full((GRID_W, GRID_W), MASKED, F32)

    r0 = pl.program_id(2) * rb
    groups = []
    for g in range(rb // ga):
        first = r0 + g * ga
        start = pl.multiple_of(jnp.clip(first - kh // 2, 0, rows - gkr) * GRID_W, GRID_W)
        kind = jnp.where(first == 0, 0, jnp.where(first == rows - ga, 2, 1))
        k = k_ref[pl.ds(start, gk), :]
        s = lax.dot_general(q_ref[g * gq:(g + 1) * gq, :], k, (((1,), (1,)), ((), ())),
                            preferred_element_type=F32)
        groups.append((start, kind, s))
    for g, (start, kind, s) in enumerate(groups):
        s = s * scale + bias_ref[kind]
        p = jnp.exp(s - jnp.max(s, axis=-1, keepdims=True))
        denom = jnp.sum(p, axis=-1, keepdims=True)
        o = jnp.dot(p.astype(v_ref.dtype), v_ref[pl.ds(start, gk), :], preferred_element_type=F32)
        o_ref[g * gq:(g + 1) * gq, :] = (o / denom).astype(o_ref.dtype)


def _attention(qkv, col_bias, nb, seq):
    n_heads, head_dim = qkv.shape[0] // 3, qkv.shape[2]
    rows = seq // GRID_W
    nr = col_bias.shape[1]
    win_rows = (nr + 1) // 2
    rb, ga, kh, gkr = _attn_geometry(rows, win_rows)
    nrb = rows // rb
    tq = rb * GRID_W
    return pl.pallas_call(
        functools.partial(_attn_kernel, rows=rows, win_rows=win_rows, scale=head_dim ** -0.5),
        grid=(n_heads, nb, nrb),
        in_specs=[pl.BlockSpec((None, tq, head_dim), lambda h, b, r: (h, b * nrb + r, 0)),
                  pl.BlockSpec((None, seq, head_dim), lambda h, b, r: (n_heads + h, b, 0)),
                  pl.BlockSpec((None, seq, head_dim), lambda h, b, r: (2 * n_heads + h, b, 0)),
                  pl.BlockSpec((None, nr, GRID_W, GRID_W), lambda h, b, r: (h, 0, 0, 0))],
        out_specs=pl.BlockSpec((tq, head_dim), lambda h, b, r: (b * nrb + r, h)),
        out_shape=jax.ShapeDtypeStruct((nb * seq, n_heads * head_dim), BF16),
        scratch_shapes=[pltpu.VMEM((3, ga * GRID_W, gkr * GRID_W), F32)],
        compiler_params=_cparams("arbitrary", "arbitrary", "arbitrary"),
        name="na2d_attention",
    )(qkv, qkv, qkv, col_bias)


def _conv_kernel(cb_ref, cc_ref, ch_ref, ccp_ref, chp_ref, ccn_ref, chn_ref, w_ref, o_ref, *, per_seq):
    i = pl.program_id(0)
    ts = cc_ref.shape[0]
    u = cc_ref[...].astype(F32) * ch_ref[...].astype(F32)
    last_halo = BF16_ROWS - 1
    u_before = ccp_ref[last_halo:, :].astype(F32) * chp_ref[last_halo:, :].astype(F32)
    u_after = ccn_ref[0:1, :].astype(F32) * chn_ref[0:1, :].astype(F32)
    u_before = jnp.where(i % per_seq == 0, 0.0, u_before)
    u_after = jnp.where(i % per_seq == per_seq - 1, 0.0, u_after)
    row = lax.broadcasted_iota(jnp.int32, u.shape, 0)
    u_prev = jnp.where(row == 0, u_before, pltpu.roll(u, 1, axis=0))
    u_next = jnp.where(row == ts - 1, u_after, pltpu.roll(u, ts - 1, axis=0))
    y = w_ref[0:1, :] * u_prev + w_ref[1:2, :] * u + w_ref[2:3, :] * u_next
    o_ref[...] = (cb_ref[...].astype(F32) * y).astype(o_ref.dtype)


def _short_conv(proj, conv_w, seq, col0, cw, ts=1024, tc=1024):
    t = proj.shape[0]
    ts, tc = _tile(seq, ts), _tile(np.gcd(cw, col0), tc)
    per_seq = seq // ts
    c0 = col0 // tc
    nct = cw // tc
    hb = ts // BF16_ROWS
    last_hb = t // BF16_ROWS - 1

    def main(k):
        return pl.BlockSpec((ts, tc), lambda i, j: (i, c0 + k * nct + j))

    def before(k):
        return pl.BlockSpec((BF16_ROWS, tc), lambda i, j: (jnp.maximum(i * hb - 1, 0), c0 + k * nct + j))

    def after(k):
        return pl.BlockSpec((BF16_ROWS, tc), lambda i, j: (jnp.minimum((i + 1) * hb, last_hb), c0 + k * nct + j))

    return pl.pallas_call(
        functools.partial(_conv_kernel, per_seq=per_seq),
        grid=(t // ts, nct),
        in_specs=[main(0), main(1), main(2), before(1), before(2), after(1), after(2),
                  pl.BlockSpec((conv_w.shape[0], tc), lambda i, j: (0, j))],
        out_specs=pl.BlockSpec((ts, tc), lambda i, j: (i, j)),
        out_shape=jax.ShapeDtypeStruct((t, cw), BF16),
        compiler_params=_cparams("parallel", "parallel"),
        name="short_conv",
    )(proj, proj, proj, proj, proj, proj, proj, conv_w)


def _merge_kernel(a_ref, c_ref, wa_ref, wb_ref, ga_ref, gb_ref, o_ref):
    ya = jnp.dot(a_ref[...], wa_ref[...], preferred_element_type=F32)
    yb = jnp.dot(c_ref[...], wb_ref[...], preferred_element_type=F32)
    ga = jax.nn.sigmoid(ga_ref[...].astype(F32))
    gb = jax.nn.sigmoid(gb_ref[...].astype(F32))
    o_ref[...] = (ga * ya + gb * yb).astype(o_ref.dtype)


def _merge(attn, convm, wa, wb, proj, gate_col0, tm=1024, tn=1024):
    t, d = attn.shape[0], wa.shape[1]
    tm, tn = _tile(t, tm), _tile(np.gcd(d, gate_col0), tn)
    g0 = gate_col0 // tn
    nt = d // tn
    return pl.pallas_call(
        _merge_kernel,
        grid=(t // tm, nt),
        in_specs=[pl.BlockSpec((tm, attn.shape[1]), lambda i, j: (i, 0)),
                  pl.BlockSpec((tm, convm.shape[1]), lambda i, j: (i, 0)),
                  pl.BlockSpec((wa.shape[0], tn), lambda i, j: (0, j)),
                  pl.BlockSpec((wb.shape[0], tn), lambda i, j: (0, j)),
                  pl.BlockSpec((tm, tn), lambda i, j: (i, g0 + j)),
                  pl.BlockSpec((tm, tn), lambda i, j: (i, g0 + nt + j))],
        out_specs=pl.BlockSpec((tm, tn), lambda i, j: (i, j)),
        out_shape=jax.ShapeDtypeStruct((t, d), BF16),
        compiler_params=_cparams("parallel", "arbitrary"),
        name="merge_branches",
    )(attn, convm, wa, wb, proj, proj)


def _wo_kernel(m_ref, w_ref, x_ref, mod_ref, o_ref, *, gate_row):
    y = jnp.dot(m_ref[...], w_ref[...], preferred_element_type=F32)
    o_ref[...] = x_ref[...] + mod_ref[gate_row:gate_row + 1, :] * y


def _wo_residual(merged, wo, x2, mod3, gate_row, seq, tm=1024, tn=1024):
    t, d = x2.shape
    tm, tn = _tile(seq, tm), _tile(d, tn)
    per_seq = seq // tm
    return pl.pallas_call(
        functools.partial(_wo_kernel, gate_row=gate_row),
        grid=(t // tm, d // tn),
        in_specs=[pl.BlockSpec((tm, merged.shape[1]), lambda i, j: (i, 0)),
                  pl.BlockSpec((wo.shape[0], tn), lambda i, j: (0, j)),
                  pl.BlockSpec((tm, tn), lambda i, j: (i, j)),
                  pl.BlockSpec((None, mod3.shape[1], tn), lambda i, j: (i // per_seq, 0, j))],
        out_specs=pl.BlockSpec((tm, tn), lambda i, j: (i, j)),
        out_shape=jax.ShapeDtypeStruct((t, d), F32),
        compiler_params=_cparams("parallel", "arbitrary"),
        name="wo_residual",
    )(merged, wo, x2, mod3)


def _candidate_pairs():
    k = PEER_TOPK
    return [(p, q) for p in range(k) for q in range(k) if (p + 1) * (q + 1) <= k]


_RANK_CODE = 2.0 ** 100


def _top16_exact(s, iota):
    n = s.shape[0]
    rank = jnp.full(s.shape, float(PEER_TOPK), F32)
    vals = []
    for it in range(PEER_TOPK):
        m = jnp.max(s, axis=0, keepdims=True)
        idx = jnp.min(jnp.where(s == m, iota, float(n)), axis=0, keepdims=True)
        sel = iota == idx
        rank = jnp.where(sel, float(it), rank)
        s = jnp.where(sel, -jnp.inf, s)
        vals.append(m)
    return rank, vals


def _top16_fast(s):
    vals = []
    for it in range(PEER_TOPK):
        m = jnp.max(s, axis=0, keepdims=True)
        s = jnp.where(s == m, -_RANK_CODE * (1.0 + it / PEER_TOPK), s)
        vals.append(m)
    is_taken = s <= -_RANK_CODE
    rank = jnp.where(is_taken, (s * (-1.0 / _RANK_CODE) - 1.0) * PEER_TOPK, float(PEER_TOPK))
    taken = jnp.sum(jnp.where(is_taken, 1.0, 0.0), axis=0, keepdims=True)
    return rank, vals, taken


def _peer_select_kernel(q_ref, k1_ref, k2_ref, lim_ref, e1_ref, r2_ref, e2_ref, s1_ref, s2_ref, a_ref, b_ref):
    nh, n, dk = k1_ref.shape
    tm = q_ref.shape[0]
    nt = (((1,), (1,)), ((), ()))
    most_taken = jnp.zeros((1, tm), F32)
    for h in range(nh):
        q1 = q_ref[:, (2 * h) * dk:(2 * h + 1) * dk]
        q2 = q_ref[:, (2 * h + 1) * dk:(2 * h + 2) * dk]
        s1 = lax.dot_general(k1_ref[h], q1, nt, preferred_element_type=F32)
        s2 = lax.dot_general(k2_ref[h], q2, nt, preferred_element_type=F32)
        s1_ref[h] = s1
        s2_ref[h] = s2
        rank1, a, taken1 = _top16_fast(s1)
        rank2, b, taken2 = _top16_fast(s2)
        lim_ref[h] = rank1
        r2_ref[h] = rank2.astype(r2_ref.dtype)
        for p in range(PEER_TOPK):
            a_ref[p * nh + h:p * nh + h + 1, :] = a[p]
            b_ref[p * nh + h:p * nh + h + 1, :] = b[p]
        most_taken = jnp.maximum(most_taken, jnp.maximum(taken1, taken2))

    @pl.when(jnp.max(most_taken) > float(PEER_TOPK))
    def _():
        iota = lax.broadcasted_iota(jnp.int32, (n, tm), 0).astype(F32)

        def redo(h, carry):
            rank1, a = _top16_exact(s1_ref[h], iota)
            rank2, b = _top16_exact(s2_ref[h], iota)
            lim_ref[h] = rank1
            r2_ref[h] = rank2.astype(r2_ref.dtype)
            for p in range(PEER_TOPK):
                a_ref[pl.ds(p * nh + h, 1), :] = a[p]
                b_ref[pl.ds(p * nh + h, 1), :] = b[p]
            return carry

        lax.fori_loop(0, nh, redo, 0)

    big_a = [a_ref[p * nh:(p + 1) * nh, :] for p in range(PEER_TOPK)]
    big_b = [b_ref[q * nh:(q + 1) * nh, :] for q in range(PEER_TOPK)]
    pairs = _candidate_pairs()
    val = [big_a[p] + big_b[q] for p, q in pairs]
    beaten = []
    for c, (p, q) in enumerate(pairs):
        beaten.append(jnp.full((nh, tm), float((p + 1) * (q + 1) - 1), F32))
    for c1, (p1, q1) in enumerate(pairs):
        for c2 in range(c1 + 1, len(pairs)):
            p2, q2 = pairs[c2]
            if (p1 <= p2 and q1 <= q2) or (p2 <= p1 and q2 <= q1):
                continue
            first_wins = jnp.where(val[c1] >= val[c2], 1.0, 0.0)
            beaten[c2] = beaten[c2] + first_wins
            beaten[c1] = beaten[c1] + (1.0 - first_wins)
    count = [jnp.zeros((nh, tm), F32) for _ in range(PEER_TOPK)]
    z = jnp.zeros((nh, tm), F32)
    for c, (p, q) in enumerate(pairs):
        chosen = beaten[c] < float(PEER_TOPK)
        count[p] = count[p] + jnp.where(chosen, 1.0, 0.0)
        z = z + jnp.where(chosen, jnp.exp((big_a[p] - big_a[0]) + (big_b[q] - big_b[0])), 0.0)
    inv_z = 1.0 / z
    for h in range(nh):
        rank1 = lim_ref[h]
        lim = jnp.zeros((n, tm), F32)
        for p in range(PEER_TOPK):
            lim = jnp.where(rank1 == float(p), count[p][h:h + 1, :], lim)
        lim_ref[h] = lim
        e1_ref[h] = jnp.exp(s1_ref[h] - big_a[0][h:h + 1, :])
        e2_ref[h] = (jnp.exp(s2_ref[h] - big_b[0][h:h + 1, :]) * inv_z[h:h + 1, :]).astype(e2_ref.dtype)


def _peer_select(qp, k1, k2, tm=256):
    t = qp.shape[0]
    nh, n, dk = k1.shape
    out = jax.ShapeDtypeStruct((nh, n, t), F32)
    out16 = jax.ShapeDtypeStruct((nh, n, t), BF16)
    ospec = pl.BlockSpec((nh, n, tm), lambda i: (0, 0, i))
    return pl.pallas_call(
        _peer_select_kernel,
        grid=(t // tm,),
        in_specs=[pl.BlockSpec((tm, qp.shape[1]), lambda i: (i, 0)),
                  pl.BlockSpec((nh, n, dk), lambda i: (0, 0, 0)),
                  pl.BlockSpec((nh, n, dk), lambda i: (0, 0, 0))],
        out_specs=[ospec, ospec, ospec, ospec],
        out_shape=[out, out, out16, out16],
        scratch_shapes=[pltpu.VMEM((nh, n, tm), F32), pltpu.VMEM((nh, n, tm), F32),
                        pltpu.VMEM((PEER_TOPK * nh, tm), F32), pltpu.VMEM((PEER_TOPK * nh, tm), F32)],
        compiler_params=_cparams("parallel"),
        name="peer_select",
    )(qp, k1, k2)


F32_ROWS = 8
PEER_SUBTILE = 512


def _peer_dense_kernel(h_ref, u_ref, vt_ref, lim_ref, e1_ref, r2_ref, e2_ref, o_ref):
    te = u_ref.shape[0]
    nh, n, _ = r2_ref.shape
    per_sub = PEER_SUBTILE // n

    @pl.when(pl.program_id(1) == 0)
    def _():
        o_ref[...] = jnp.zeros_like(o_ref)

    def gate(c):
        g = None
        for h in range(nh):
            lim = lim_ref[h, c:c + 1, :].astype(BF16)
            e1 = e1_ref[h, c:c + 1, :].astype(BF16)
            term = jnp.where(r2_ref[h] < lim, e2_ref[h] * e1, jnp.zeros((), BF16))
            g = term if g is None else g + term
        return g

    hh = h_ref[...]
    subs = [slice(s * PEER_SUBTILE, (s + 1) * PEER_SUBTILE) for s in range(te // PEER_SUBTILE)]
    acts = [jnp.dot(u_ref[rows, :], hh, preferred_element_type=F32) for rows in subs]
    for s, rows in enumerate(subs):
        g = jnp.concatenate([gate(s * per_sub + c) for c in range(per_sub)], axis=0)
        ga = g * jax.nn.gelu(acts[s]).astype(BF16)
        o_ref[...] += jnp.dot(vt_ref[:, rows], ga, preferred_element_type=F32)


def _peer_dense(h2t, u, vt, lim, e1, r2, e2, tm=512):
    d, t = h2t.shape
    ne = u.shape[0]
    nh, n, _ = lim.shape
    tm = _tile(t, tm)
    te = F32_ROWS * n
    assert te % PEER_SUBTILE == 0 and PEER_SUBTILE % n == 0
    once = pl.Buffered(1)
    rows = pl.BlockSpec((nh, F32_ROWS, tm), lambda i, e: (0, e, i))
    full = pl.BlockSpec((nh, n, tm), lambda i, e: (0, 0, i))
    return pl.pallas_call(
        _peer_dense_kernel,
        grid=(t // tm, ne // te),
        in_specs=[pl.BlockSpec((d, tm), lambda i, e: (0, i), pipeline_mode=once),
                  pl.BlockSpec((te, d), lambda i, e: (e, 0)),
                  pl.BlockSpec((d, te), lambda i, e: (0, e)),
                  rows, rows, full, full],
        out_specs=pl.BlockSpec((d, tm), lambda i, e: (0, i), pipeline_mode=once),
        out_shape=jax.ShapeDtypeStruct((d, t), F32),
        compiler_params=_cparams("parallel", "arbitrary"),
        name="peer_dense",
    )(h2t, u, vt, lim, e1, r2, e2)


def _peer_residual_kernel(x_ref, pt_ref, mod_ref, g_ref, o_ref, *, gate_row, final_norm):
    y = x_ref[...] + mod_ref[gate_row:gate_row + 1, :] * pt_ref[...].T
    if final_norm:
        y = y * lax.rsqrt(jnp.mean(y * y, axis=-1, keepdims=True) + RMS_EPS) * g_ref[...]
    o_ref[...] = y


def _peer_residual(x1, peer_t, mod3, gate_row, g, seq, final_norm, tm=256):
    t, d = x1.shape
    per_seq = seq // tm
    return pl.pallas_call(
        functools.partial(_peer_residual_kernel, gate_row=gate_row, final_norm=final_norm),
        grid=(t // tm,),
        in_specs=[pl.BlockSpec((tm, d), lambda i: (i, 0)),
                  pl.BlockSpec((d, tm), lambda i: (0, i)),
                  pl.BlockSpec((None, mod3.shape[1], d), lambda i: (i // per_seq, 0, 0)),
                  pl.BlockSpec((1, d), lambda i: (0, 0))],
        out_specs=pl.BlockSpec((tm, d), lambda i: (i, 0)),
        out_shape=jax.ShapeDtypeStruct((t, d), F32),
        compiler_params=_cparams("parallel"),
        name="peer_residual",
    )(x1, peer_t, mod3, g.reshape(1, d))


def kernel(x, c, w_ada, b_ada, norm1_g, w_in, rpb, conv_w, w_attn_out, w_conv_out, w_o, norm2_g,
           w_q_peer, sub_keys_1, sub_keys_2, expert_u, expert_v, norm_f_g):
    nb, seq, d = x.shape
    depth = w_ada.shape[0]
    n_mod = w_ada.shape[2] // d
    aw = w_attn_out.shape[1]
    cw = w_conv_out.shape[1]
    x2 = x.reshape(nb * seq, d)
    for l in range(depth):
        mod3 = _mod(c, w_ada[l], b_ada[l]).reshape(nb, n_mod, d)

        h1 = _norm_mod(x2, norm1_g[l], mod3, 0, seq)
        qkv, w_in16, wa16, wc16, wo16, wq16 = _matmul(
            h1, w_in[l][:, :3 * aw].astype(BF16), 1024, 512, "in_proj_qkv", chunked=True,
            cast_jobs=[(w_in[l], False), (w_attn_out[l], False), (w_conv_out[l], False),
                       (w_o[l], False), (w_q_peer[l], False)])
        rest, u16, vt16 = _matmul(h1, w_in16, 1024, 1024, "in_proj_rest", col0=3 * aw,
                                  cast_jobs=[(expert_u[l], False), (expert_v[l], True)])
        attn = _attention(qkv, _attn_col_bias(rpb[l]), nb, seq)
        convm = _short_conv(rest, conv_w[l], seq, 0, cw)
        merged = _merge(attn, convm, wa16, wc16, rest, 3 * cw)
        x2 = _wo_residual(merged, wo16, x2, mod3, 2, seq)

        h2, h2t = _norm_mod(x2, norm2_g[l], mod3, 3, seq, with_transposed=True)
        qp = _matmul(h2, wq16, 1024, 1024, "peer_query")
        lim, e1, r2, e2 = _peer_select(qp, sub_keys_1[l].astype(BF16), sub_keys_2[l].astype(BF16))
        peer_t = _peer_dense(h2t, u16, vt16, lim, e1, r2, e2)
        x2 = _peer_residual(x2, peer_t, mod3, 5, norm_f_g, seq, final_norm=(l == depth - 1))
    return x2.reshape(nb, seq, d)
```

```python
import functools

import numpy as np
import jax
import jax.numpy as jnp
from jax import lax
from jax.experimental import pallas as pl
from jax.experimental.pallas import tpu as pltpu

GRID_W = 64
RMS_EPS = 1e-6
PEER_TOPK = 16
LANES = 128
BF16_ROWS = 16
V7X_VMEM_BYTES = 64 * 1024 * 1024
VMEM_LIMIT = V7X_VMEM_BYTES - 8 * 1024 * 1024
ATTN_ROW_BLOCK = 32
ATTN_GROUP_ROWS = 4
MASKED = -1e30

F32 = jnp.float32
BF16 = jnp.bfloat16


def _cparams(*sem):
    return pltpu.CompilerParams(dimension_semantics=sem, vmem_limit_bytes=VMEM_LIMIT)


def _tile(n, target, align=LANES):
    best = None
    for cand in range(align, min(n, target) + 1, align):
        if n % cand == 0:
            best = cand
    assert best is not None, (n, target, align)
    return best


def _mod_kernel(cb_ref, w_ref, b_ref, o_ref):
    nb = cb_ref.shape[0]
    tn = w_ref.shape[1]
    for j in range(tn // LANES):
        cols = slice(j * LANES, (j + 1) * LANES)
        w = w_ref[:, cols]
        for b in range(nb):
            o_ref[b:b + 1, cols] = jnp.sum(w * cb_ref[b], axis=0, keepdims=True) + b_ref[:, cols]


def _mod(c, w_ada, b_ada, tn=512):
    nb, d = c.shape
    n = w_ada.shape[1]
    cb = jnp.broadcast_to(c[:, :, None], (nb, d, LANES))
    return pl.pallas_call(
        _mod_kernel,
        grid=(n // tn,),
        in_specs=[pl.BlockSpec((nb, d, LANES), lambda j: (0, 0, 0)),
                  pl.BlockSpec((d, tn), lambda j: (0, j)),
                  pl.BlockSpec((1, tn), lambda j: (0, j))],
        out_specs=pl.BlockSpec((nb, tn), lambda j: (0, j)),
        out_shape=jax.ShapeDtypeStruct((nb, n), F32),
        compiler_params=_cparams("arbitrary"),
        name="ada_mod",
    )(cb, w_ada, b_ada.reshape(1, n))


def _norm_mod_kernel(x_ref, g_ref, mod_ref, o_ref, *, shift_row, transposed):
    x = x_ref[...]
    y = x * lax.rsqrt(jnp.mean(x * x, axis=-1, keepdims=True) + RMS_EPS) * g_ref[...]
    shift = mod_ref[shift_row:shift_row + 1, :]
    scale = mod_ref[shift_row + 1:shift_row + 2, :]
    h = y * (1.0 + scale) + shift
    o_ref[...] = (h.T if transposed else h).astype(o_ref.dtype)


def _norm_mod(x2, g, mod3, shift_row, seq, transposed=False, tm=512):
    t, d = x2.shape
    tm = _tile(seq, tm)
    per_seq = seq // tm
    if transposed:
        out_spec, out_shape = pl.BlockSpec((d, tm), lambda i: (0, i)), jax.ShapeDtypeStruct((d, t), BF16)
    else:
        out_spec, out_shape = pl.BlockSpec((tm, d), lambda i: (i, 0)), jax.ShapeDtypeStruct((t, d), BF16)
    return pl.pallas_call(
        functools.partial(_norm_mod_kernel, shift_row=shift_row, transposed=transposed),
        grid=(t // tm,),
        in_specs=[pl.BlockSpec((tm, d), lambda i: (i, 0)),
                  pl.BlockSpec((1, d), lambda i: (0, 0)),
                  pl.BlockSpec((None, mod3.shape[1], d), lambda i: (i // per_seq, 0, 0))],
        out_specs=out_spec,
        out_shape=out_shape,
        compiler_params=_cparams("parallel"),
        name="norm_mod",
    )(x2, g.reshape(1, d), mod3)


def _mm_kernel(a_ref, w_ref, *refs, chunked, transposed):
    n_side = len(transposed)
    o_ref = refs[n_side]
    r = jnp.dot(a_ref[...], w_ref[...], preferred_element_type=F32)
    if chunked:
        for c in range(o_ref.shape[0]):
            o_ref[c] = r[:, c * LANES:(c + 1) * LANES].astype(o_ref.dtype)
    else:
        o_ref[...] = r.astype(o_ref.dtype)
    for src, dst, tr in zip(refs[:n_side], refs[n_side + 1:], transposed):
        dst[...] = (src[...].T if tr else src[...]).astype(dst.dtype)


def _matmul(a, w, tm, tn, name, col0=0, n=None, chunked=False, cast_jobs=()):
    m, k = a.shape
    n = w.shape[1] - col0 if n is None else n
    tm, tn = _tile(m, tm), _tile(int(np.gcd(n, col0)), tn)
    j0 = col0 // tn
    grid = (m // tm, n // tn)
    steps = grid[0] * grid[1]
    in_specs = [pl.BlockSpec((tm, k), lambda i, j: (i, 0)),
                pl.BlockSpec((k, tn), lambda i, j: (0, j0 + j))]
    if chunked:
        out_specs = [pl.BlockSpec((tn // LANES, tm, LANES), lambda i, j: (j, i, 0))]
        out_shape = [jax.ShapeDtypeStruct((n // LANES, m, LANES), BF16)]
    else:
        out_specs = [pl.BlockSpec((tm, tn), lambda i, j: (i, j))]
        out_shape = [jax.ShapeDtypeStruct((m, n), BF16)]
    for table, tr in cast_jobs:
        nr, nc = table.shape
        align = LANES if tr else BF16_ROWS
        rows = min(r for r in range(align, nr + 1, align) if nr % r == 0 and nr // r <= steps)

        def block(i, j, last=nr // rows - 1):
            return jnp.minimum(i * grid[1] + j, last)

        in_specs.append(pl.BlockSpec((rows, nc), lambda i, j, block=block: (block(i, j), 0)))
        if tr:
            out_specs.append(pl.BlockSpec((nc, rows), lambda i, j, block=block: (0, block(i, j))))
            out_shape.append(jax.ShapeDtypeStruct((nc, nr), BF16))
        else:
            out_specs.append(pl.BlockSpec((rows, nc), lambda i, j, block=block: (block(i, j), 0)))
            out_shape.append(jax.ShapeDtypeStruct((nr, nc), BF16))
    out = pl.pallas_call(
        functools.partial(_mm_kernel, chunked=chunked, transposed=tuple(tr for _, tr in cast_jobs)),
        grid=grid, in_specs=in_specs, out_specs=out_specs, out_shape=out_shape,
        compiler_params=_cparams("arbitrary", "arbitrary"), name=name,
    )(a, w, *[table for table, _ in cast_jobs])
    return out if cast_jobs else out[0]


def _attn_geometry(rows, win_rows):
    ga = ATTN_GROUP_ROWS
    kh = min(win_rows, rows)
    gkr = min(ga + kh, rows)
    rb = min(ATTN_ROW_BLOCK, rows)
    assert rows % rb == 0 and rb % ga == 0 and ga >= kh // 2, (rows, rb, ga, kh)
    return rb, ga, kh, gkr


def _attn_col_bias(rpb):
    nh, nr, nc = rpb.shape
    win_cols = (nc + 1) // 2
    cols = np.arange(GRID_W)
    cs = np.clip(cols - win_cols // 2, 0, GRID_W - win_cols)
    kc = cols[None, :]
    cvalid = (kc >= cs[:, None]) & (kc < cs[:, None] + win_cols)
    dc = np.clip(kc - cols[:, None] + (win_cols - 1), 0, nc - 1)
    onehot = (dc[None] == np.arange(nc)[:, None, None]).astype(np.float32)
    cb = jnp.einsum('hrm,mqc->hrqc', rpb, jnp.asarray(onehot), precision=lax.Precision.HIGHEST)
    return jnp.where(jnp.asarray(cvalid), cb, MASKED)


def _attn_kernel(q_ref, k_ref, v_ref, cb_ref, o_ref, bias_ref, *, rows, win_rows, scale):
    rb, ga, kh, gkr = _attn_geometry(rows, win_rows)
    gq, gk = ga * GRID_W, gkr * GRID_W
    first_rows = (0, min(ga, rows - ga), rows - ga)

    @pl.when((pl.program_id(1) == 0) & (pl.program_id(2) == 0))
    def _():
        for kind, first in enumerate(first_rows):
            ws = int(np.clip(first - kh // 2, 0, rows - gkr))
            for a in range(ga):
                r = first + a
                rs = int(np.clip(r - kh // 2, 0, rows - kh))
                for i in range(gkr):
                    krow = ws + i
                    dst = (kind, slice(a * GRID_W, (a + 1) * GRID_W), slice(i * GRID_W, (i + 1) * GRID_W))
                    if rs <= krow < rs + kh:
                        bias_ref[dst] = cb_ref[krow - r + (win_rows - 1)]
                    else:
                        bias_ref[dst] = jnp.full((GRID_W, GRID_W), MASKED, F32)

    r0 = pl.program_id(2) * rb
    groups = []
    for g in range(rb // ga):
        first = r0 + g * ga
        start = pl.multiple_of(jnp.clip(first - kh // 2, 0, rows - gkr) * GRID_W, GRID_W)
        kind = jnp.where(first == 0, 0, jnp.where(first == rows - ga, 2, 1))
        k = k_ref[pl.ds(start, gk), :]
        s = lax.dot_general(q_ref[g * gq:(g + 1) * gq, :], k, (((1,), (1,)), ((), ())),
                            preferred_element_type=F32)
        groups.append((start, kind, s))
    for g, (start, kind, s) in enumerate(groups):
        s = s * scale + bias_ref[kind]
        p = jnp.exp(s - jnp.max(s, axis=-1, keepdims=True))
        denom = jnp.sum(p, axis=-1, keepdims=True)
        o = jnp.dot(p.astype(v_ref.dtype), v_ref[pl.ds(start, gk), :], preferred_element_type=F32)
        o_ref[g * gq:(g + 1) * gq, :] = (o / denom).astype(o_ref.dtype)


def _attention(qkv, col_bias, nb, seq):
    n_heads, head_dim = qkv.shape[0] // 3, qkv.shape[2]
    rows = seq // GRID_W
    nr = col_bias.shape[1]
    win_rows = (nr + 1) // 2
    rb, ga, kh, gkr = _attn_geometry(rows, win_rows)
    nrb = rows // rb
    tq = rb * GRID_W
    return pl.pallas_call(
        functools.partial(_attn_kernel, rows=rows, win_rows=win_rows, scale=head_dim ** -0.5),
        grid=(n_heads, nb, nrb),
        in_specs=[pl.BlockSpec((None, tq, head_dim), lambda h, b, r: (h, b * nrb + r, 0)),
                  pl.BlockSpec((None, seq, head_dim), lambda h, b, r: (n_heads + h, b, 0)),
                  pl.BlockSpec((None, seq, head_dim), lambda h, b, r: (2 * n_heads + h, b, 0)),
                  pl.BlockSpec((None, nr, GRID_W, GRID_W), lambda h, b, r: (h, 0, 0, 0))],
        out_specs=pl.BlockSpec((tq, head_dim), lambda h, b, r: (b * nrb + r, h)),
        out_shape=jax.ShapeDtypeStruct((nb * seq, n_heads * head_dim), BF16),
        scratch_shapes=[pltpu.VMEM((3, ga * GRID_W, gkr * GRID_W), F32)],
        compiler_params=_cparams("arbitrary", "arbitrary", "arbitrary"),
        name="na2d_attention",
    )(qkv, qkv, qkv, col_bias)


def _conv_kernel(cb_ref, cc_ref, ch_ref, ccp_ref, chp_ref, ccn_ref, chn_ref, w_ref, o_ref, *, per_seq):
    i = pl.program_id(0)
    ts = cc_ref.shape[0]
    u = cc_ref[...].astype(F32) * ch_ref[...].astype(F32)
    last_halo = BF16_ROWS - 1
    u_before = ccp_ref[last_halo:, :].astype(F32) * chp_ref[last_halo:, :].astype(F32)
    u_after = ccn_ref[0:1, :].astype(F32) * chn_ref[0:1, :].astype(F32)
    u_before = jnp.where(i % per_seq == 0, 0.0, u_before)
    u_after = jnp.where(i % per_seq == per_seq - 1, 0.0, u_after)
    row = lax.broadcasted_iota(jnp.int32, u.shape, 0)
    u_prev = jnp.where(row == 0, u_before, pltpu.roll(u, 1, axis=0))
    u_next = jnp.where(row == ts - 1, u_after, pltpu.roll(u, ts - 1, axis=0))
    y = w_ref[0:1, :] * u_prev + w_ref[1:2, :] * u + w_ref[2:3, :] * u_next
    o_ref[...] = (cb_ref[...].astype(F32) * y).astype(o_ref.dtype)


def _short_conv(proj, conv_w, seq, col0, cw, ts=1024, tc=1024):
    t = proj.shape[0]
    ts, tc = _tile(seq, ts), _tile(np.gcd(cw, col0), tc)
    per_seq = seq // ts
    c0 = col0 // tc
    nct = cw // tc
    hb = ts // BF16_ROWS
    last_hb = t // BF16_ROWS - 1

    def main(k):
        return pl.BlockSpec((ts, tc), lambda i, j: (i, c0 + k * nct + j))

    def before(k):
        return pl.BlockSpec((BF16_ROWS, tc), lambda i, j: (jnp.maximum(i * hb - 1, 0), c0 + k * nct + j))

    def after(k):
        return pl.BlockSpec((BF16_ROWS, tc), lambda i, j: (jnp.minimum((i + 1) * hb, last_hb), c0 + k * nct + j))

    return pl.pallas_call(
        functools.partial(_conv_kernel, per_seq=per_seq),
        grid=(t // ts, nct),
        in_specs=[main(0), main(1), main(2), before(1), before(2), after(1), after(2),
                  pl.BlockSpec((conv_w.shape[0], tc), lambda i, j: (0, j))],
        out_specs=pl.BlockSpec((ts, tc), lambda i, j: (i, j)),
        out_shape=jax.ShapeDtypeStruct((t, cw), BF16),
        compiler_params=_cparams("parallel", "parallel"),
        name="short_conv",
    )(proj, proj, proj, proj, proj, proj, proj, conv_w)


def _merge_kernel(a_ref, c_ref, wa_ref, wb_ref, ga_ref, gb_ref, o_ref):
    ya = jnp.dot(a_ref[...], wa_ref[...], preferred_element_type=F32)
    yb = jnp.dot(c_ref[...], wb_ref[...], preferred_element_type=F32)
    ga = jax.nn.sigmoid(ga_ref[...].astype(F32))
    gb = jax.nn.sigmoid(gb_ref[...].astype(F32))
    o_ref[...] = (ga * ya + gb * yb).astype(o_ref.dtype)


def _merge(attn, convm, wa, wb, proj, gate_col0, tm=1024, tn=1024):
    t, d = attn.shape[0], wa.shape[1]
    tm, tn = _tile(t, tm), _tile(np.gcd(d, gate_col0), tn)
    g0 = gate_col0 // tn
    nt = d // tn
    return pl.pallas_call(
        _merge_kernel,
        grid=(t // tm, nt),
        in_specs=[pl.BlockSpec((tm, attn.shape[1]), lambda i, j: (i, 0)),
                  pl.BlockSpec((tm, convm.shape[1]), lambda i, j: (i, 0)),
                  pl.BlockSpec((wa.shape[0], tn), lambda i, j: (0, j)),
                  pl.BlockSpec((wb.shape[0], tn), lambda i, j: (0, j)),
                  pl.BlockSpec((tm, tn), lambda i, j: (i, g0 + j)),
                  pl.BlockSpec((tm, tn), lambda i, j: (i, g0 + nt + j))],
        out_specs=pl.BlockSpec((tm, tn), lambda i, j: (i, j)),
        out_shape=jax.ShapeDtypeStruct((t, d), BF16),
        compiler_params=_cparams("parallel", "arbitrary"),
        name="merge_branches",
    )(attn, convm, wa, wb, proj, proj)


def _wo_kernel(m_ref, w_ref, x_ref, mod_ref, o_ref, *, gate_row):
    y = jnp.dot(m_ref[...], w_ref[...], preferred_element_type=F32)
    o_ref[...] = x_ref[...] + mod_ref[gate_row:gate_row + 1, :] * y


def _wo_residual(merged, wo, x2, mod3, gate_row, seq, tm=1024, tn=1024):
    t, d = x2.shape
    tm, tn = _tile(seq, tm), _tile(d, tn)
    per_seq = seq // tm
    return pl.pallas_call(
        functools.partial(_wo_kernel, gate_row=gate_row),
        grid=(t // tm, d // tn),
        in_specs=[pl.BlockSpec((tm, merged.shape[1]), lambda i, j: (i, 0)),
                  pl.BlockSpec((wo.shape[0], tn), lambda i, j: (0, j)),
                  pl.BlockSpec((tm, tn), lambda i, j: (i, j)),
                  pl.BlockSpec((None, mod3.shape[1], tn), lambda i, j: (i // per_seq, 0, j))],
        out_specs=pl.BlockSpec((tm, tn), lambda i, j: (i, j)),
        out_shape=jax.ShapeDtypeStruct((t, d), F32),
        compiler_params=_cparams("parallel", "arbitrary"),
        name="wo_residual",
    )(merged, wo, x2, mod3)


def _candidate_pairs():
    k = PEER_TOPK
    return [(p, q) for p in range(k) for q in range(k) if (p + 1) * (q + 1) <= k]


_RANK_CODE = 2.0 ** 100


def _top16_exact(s, iota):
    n = s.shape[0]
    rank = jnp.full(s.shape, float(PEER_TOPK), F32)
    vals = []
    for it in range(PEER_TOPK):
        m = jnp.max(s, axis=0, keepdims=True)
        idx = jnp.min(jnp.where(s == m, iota, float(n)), axis=0, keepdims=True)
        sel = iota == idx
        rank = jnp.where(sel, float(it), rank)
        s = jnp.where(sel, -jnp.inf, s)
        vals.append(m)
    return rank, vals


def _top16_fast(s):
    vals = []
    for it in range(PEER_TOPK):
        m = jnp.max(s, axis=0, keepdims=True)
        s = jnp.where(s == m, -_RANK_CODE * (1.0 + it / PEER_TOPK), s)
        vals.append(m)
    is_taken = s <= -_RANK_CODE
    rank = jnp.where(is_taken, (s * (-1.0 / _RANK_CODE) - 1.0) * PEER_TOPK, float(PEER_TOPK))
    taken = jnp.sum(jnp.where(is_taken, 1.0, 0.0), axis=0, keepdims=True)
    return rank, vals, taken


def _peer_select_kernel(h_ref, wq_ref, k1_ref, k2_ref, lim_ref, e1_ref, r2_ref, e2_ref,
                        s1_ref, s2_ref, a_ref, b_ref):
    nh, n, dk = k1_ref.shape
    tm = h_ref.shape[1]

    def queries(h):
        return jnp.dot(wq_ref[2 * h * dk:2 * (h + 1) * dk, :], h_ref[...],
                       preferred_element_type=F32).astype(BF16)

    most_taken = jnp.zeros((1, tm), F32)
    q_head = queries(0)
    for h in range(nh):
        q1, q2 = q_head[:dk], q_head[dk:]
        if h + 1 < nh:
            q_head = queries(h + 1)
        s1 = jnp.dot(k1_ref[h], q1, preferred_element_type=F32)
        s2 = jnp.dot(k2_ref[h], q2, preferred_element_type=F32)
        s1_ref[h] = s1
        s2_ref[h] = s2
        rank1, a, taken1 = _top16_fast(s1)
        rank2, b, taken2 = _top16_fast(s2)
        lim_ref[h] = rank1
        r2_ref[h] = rank2.astype(r2_ref.dtype)
        for p in range(PEER_TOPK):
            a_ref[p * nh + h:p * nh + h + 1, :] = a[p]
            b_ref[p * nh + h:p * nh + h + 1, :] = b[p]
        most_taken = jnp.maximum(most_taken, jnp.maximum(taken1, taken2))

    @pl.when(jnp.max(most_taken) > float(PEER_TOPK))
    def _():
        iota = lax.broadcasted_iota(jnp.int32, (n, tm), 0).astype(F32)

        def redo(h, carry):
            rank1, a = _top16_exact(s1_ref[h], iota)
            rank2, b = _top16_exact(s2_ref[h], iota)
            lim_ref[h] = rank1
            r2_ref[h] = rank2.astype(r2_ref.dtype)
            for p in range(PEER_TOPK):
                a_ref[pl.ds(p * nh + h, 1), :] = a[p]
                b_ref[pl.ds(p * nh + h, 1), :] = b[p]
            return carry

        lax.fori_loop(0, nh, redo, 0)

    big_a = [a_ref[p * nh:(p + 1) * nh, :] for p in range(PEER_TOPK)]
    big_b = [b_ref[q * nh:(q + 1) * nh, :] for q in range(PEER_TOPK)]
    pairs = _candidate_pairs()
    val = [big_a[p] + big_b[q] for p, q in pairs]
    beaten = []
    for c, (p, q) in enumerate(pairs):
        beaten.append(jnp.full((nh, tm), float((p + 1) * (q + 1) - 1), F32))
    for c1, (p1, q1) in enumerate(pairs):
        for c2 in range(c1 + 1, len(pairs)):
            p2, q2 = pairs[c2]
            if (p1 <= p2 and q1 <= q2) or (p2 <= p1 and q2 <= q1):
                continue
            first_wins = jnp.where(val[c1] >= val[c2], 1.0, 0.0)
            beaten[c2] = beaten[c2] + first_wins
            beaten[c1] = beaten[c1] + (1.0 - first_wins)
    count = [jnp.zeros((nh, tm), F32) for _ in range(PEER_TOPK)]
    z = jnp.zeros((nh, tm), F32)
    for c, (p, q) in enumerate(pairs):
        chosen = beaten[c] < float(PEER_TOPK)
        count[p] = count[p] + jnp.where(chosen, 1.0, 0.0)
        z = z + jnp.where(chosen, jnp.exp((big_a[p] - big_a[0]) + (big_b[q] - big_b[0])), 0.0)
    inv_z = 1.0 / z
    for h in range(nh):
        rank1 = lim_ref[h]
        lim = jnp.zeros((n, tm), F32)
        for p in range(PEER_TOPK):
            lim = jnp.where(rank1 == float(p), count[p][h:h + 1, :], lim)
        lim_ref[h] = lim
        e1_ref[h] = jnp.exp(s1_ref[h] - big_a[0][h:h + 1, :])
        e2_ref[h] = (jnp.exp(s2_ref[h] - big_b[0][h:h + 1, :]) * inv_z[h:h + 1, :]).astype(e2_ref.dtype)


def _peer_select(h2t, wq_t, k1, k2, tm=256):
    d, t = h2t.shape
    nh, n, dk = k1.shape
    out = jax.ShapeDtypeStruct((nh, n, t), F32)
    out16 = jax.ShapeDtypeStruct((nh, n, t), BF16)
    ospec = pl.BlockSpec((nh, n, tm), lambda i: (0, 0, i))
    return pl.pallas_call(
        _peer_select_kernel,
        grid=(t // tm,),
        in_specs=[pl.BlockSpec((d, tm), lambda i: (0, i)),
                  pl.BlockSpec(wq_t.shape, lambda i: (0, 0), pipeline_mode=pl.Buffered(1)),
                  pl.BlockSpec((nh, n, dk), lambda i: (0, 0, 0)),
                  pl.BlockSpec((nh, n, dk), lambda i: (0, 0, 0))],
        out_specs=[ospec, ospec, ospec, ospec],
        out_shape=[out, out, out16, out16],
        scratch_shapes=[pltpu.VMEM((nh, n, tm), F32), pltpu.VMEM((nh, n, tm), F32),
                        pltpu.VMEM((PEER_TOPK * nh, tm), F32), pltpu.VMEM((PEER_TOPK * nh, tm), F32)],
        compiler_params=_cparams("parallel"),
        name="peer_select",
    )(h2t, wq_t, k1, k2)


F32_ROWS = 8
PEER_SUBTILE = 512


def _peer_dense_kernel(h_ref, u_ref, vt_ref, lim_ref, e1_ref, r2_ref, e2_ref, o_ref):
    te = u_ref.shape[0]
    nh, n, _ = r2_ref.shape
    per_sub = PEER_SUBTILE // n

    @pl.when(pl.program_id(1) == 0)
    def _():
        o_ref[...] = jnp.zeros_like(o_ref)

    def gate(c):
        g = None
        for h in range(nh):
            lim = lim_ref[h, c:c + 1, :].astype(BF16)
            e1 = e1_ref[h, c:c + 1, :].astype(BF16)
            term = jnp.where(r2_ref[h] < lim, e2_ref[h] * e1, jnp.zeros((), BF16))
            g = term if g is None else g + term
        return g

    hh = h_ref[...]
    subs = [slice(s * PEER_SUBTILE, (s + 1) * PEER_SUBTILE) for s in range(te // PEER_SUBTILE)]
    acts = [jnp.dot(u_ref[rows, :], hh, preferred_element_type=F32) for rows in subs]
    for s, rows in enumerate(subs):
        g = jnp.concatenate([gate(s * per_sub + c) for c in range(per_sub)], axis=0)
        ga = g * jax.nn.gelu(acts[s]).astype(BF16)
        o_ref[...] += jnp.dot(vt_ref[:, rows], ga, preferred_element_type=F32)


def _peer_dense(h2t, u, vt, lim, e1, r2, e2, tm=512):
    d, t = h2t.shape
    ne = u.shape[0]
    nh, n, _ = lim.shape
    tm = _tile(t, tm)
    te = F32_ROWS * n
    assert te % PEER_SUBTILE == 0 and PEER_SUBTILE % n == 0
    once = pl.Buffered(1)
    rows = pl.BlockSpec((nh, F32_ROWS, tm), lambda i, e: (0, e, i))
    full = pl.BlockSpec((nh, n, tm), lambda i, e: (0, 0, i))
    return pl.pallas_call(
        _peer_dense_kernel,
        grid=(t // tm, ne // te),
        in_specs=[pl.BlockSpec((d, tm), lambda i, e: (0, i), pipeline_mode=once),
                  pl.BlockSpec((te, d), lambda i, e: (e, 0)),
                  pl.BlockSpec((d, te), lambda i, e: (0, e)),
                  rows, rows, full, full],
        out_specs=pl.BlockSpec((d, tm), lambda i, e: (0, i), pipeline_mode=once),
        out_shape=jax.ShapeDtypeStruct((d, t), F32),
        compiler_params=_cparams("parallel", "arbitrary"),
        name="peer_dense",
    )(h2t, u, vt, lim, e1, r2, e2)


def _peer_residual_kernel(x_ref, pt_ref, mod_ref, g_ref, o_ref, *, gate_row, final_norm):
    y = x_ref[...] + mod_ref[gate_row:gate_row + 1, :] * pt_ref[...].T
    if final_norm:
        y = y * lax.rsqrt(jnp.mean(y * y, axis=-1, keepdims=True) + RMS_EPS) * g_ref[...]
    o_ref[...] = y


def _peer_residual(x1, peer_t, mod3, gate_row, g, seq, final_norm, tm=256):
    t, d = x1.shape
    per_seq = seq // tm
    return pl.pallas_call(
        functools.partial(_peer_residual_kernel, gate_row=gate_row, final_norm=final_norm),
        grid=(t // tm,),
        in_specs=[pl.BlockSpec((tm, d), lambda i: (i, 0)),
                  pl.BlockSpec((d, tm), lambda i: (0, i)),
                  pl.BlockSpec((None, mod3.shape[1], d), lambda i: (i // per_seq, 0, 0)),
                  pl.BlockSpec((1, d), lambda i: (0, 0))],
        out_specs=pl.BlockSpec((tm, d), lambda i: (i, 0)),
        out_shape=jax.ShapeDtypeStruct((t, d), F32),
        compiler_params=_cparams("parallel"),
        name="peer_residual",
    )(x1, peer_t, mod3, g.reshape(1, d))


def kernel(x, c, w_ada, b_ada, norm1_g, w_in, rpb, conv_w, w_attn_out, w_conv_out, w_o, norm2_g,
           w_q_peer, sub_keys_1, sub_keys_2, expert_u, expert_v, norm_f_g):
    nb, seq, d = x.shape
    depth = w_ada.shape[0]
    n_mod = w_ada.shape[2] // d
    aw = w_attn_out.shape[1]
    cw = w_conv_out.shape[1]
    x2 = x.reshape(nb * seq, d)
    for l in range(depth):
        mod3 = _mod(c, w_ada[l], b_ada[l]).reshape(nb, n_mod, d)

        h1 = _norm_mod(x2, norm1_g[l], mod3, 0, seq)
        qkv, w_in16, wa16, wc16, wo16, wq16t = _matmul(
            h1, w_in[l][:, :3 * aw].astype(BF16), 1024, 512, "in_proj_qkv", chunked=True,
            cast_jobs=[(w_in[l], False), (w_attn_out[l], False), (w_conv_out[l], False),
                       (w_o[l], False), (w_q_peer[l], True)])
        rest, u16, vt16 = _matmul(h1, w_in16, 1024, 1024, "in_proj_rest", col0=3 * aw,
                                  cast_jobs=[(expert_u[l], False), (expert_v[l], True)])
        attn = _attention(qkv, _attn_col_bias(rpb[l]), nb, seq)
        convm = _short_conv(rest, conv_w[l], seq, 0, cw)
        merged = _merge(attn, convm, wa16, wc16, rest, 3 * cw)
        x2 = _wo_residual(merged, wo16, x2, mod3, 2, seq)

        h2t = _norm_mod(x2, norm2_g[l], mod3, 3, seq, transposed=True)
        lim, e1, r2, e2 = _peer_select(h2t, wq16t, sub_keys_1[l].astype(BF16), sub_keys_2[l].astype(BF16))
        peer_t = _peer_dense(h2t, u16, vt16, lim, e1, r2, e2)
        x2 = _peer_residual(x2, peer_t, mod3, 5, norm_f_g, seq, final_norm=(l == depth - 1))
    return x2.reshape(nb, seq, d)
```
